```python
import math
import jax, jax.numpy as jnp
from jax import lax
import numpy as np

D_MODEL = 1024
BATCH = 8
SEQ = 2048
DEPTH = 4

CHUNK = 64
N_MIXERS = 3
ALPHA = (2.0 * DEPTH) ** 0.25
BETA = (8.0 * DEPTH) ** -0.25
LN_EPS = 1e-5
RMS_EPS = 1e-6
SC_WIDTH = 3
MLA_HEADS = 8
QK_NOPE = 128
QK_ROPE = 64
V_HEAD = 128
Q_LORA = 3 * D_MODEL // 8
KV_LORA = D_MODEL // 4
ROPE_THETA = 10000.0
Q_BLOCK = 128
CONF_WIDTH = 31
D_FF = 4 * D_MODEL
N_A = len(range(0, DEPTH, N_MIXERS))
N_B = len(range(1, DEPTH, N_MIXERS))
N_C = len(range(2, DEPTH, N_MIXERS))

kernel_name = "hybrid_chunk_causal_deepnorm_trunk"


def _layer_norm(x, g, b):
    xf = x.astype(jnp.float32)
    mu = jnp.mean(xf, axis=-1, keepdims=True)
    var = jnp.mean(jnp.square(xf - mu), axis=-1, keepdims=True)
    y = (xf - mu) * lax.rsqrt(var + LN_EPS) * g.astype(jnp.float32) + b.astype(jnp.float32)
    return y.astype(x.dtype)


def _rms_norm(x, g):
    xf = x.astype(jnp.float32)
    y = xf * lax.rsqrt(jnp.mean(jnp.square(xf), axis=-1, keepdims=True) + RMS_EPS) * g.astype(jnp.float32)
    return y.astype(x.dtype)


def _causal_dwconv(x, w):
    k_width, c = w.shape
    return lax.conv_general_dilated(
        x, w[:, None, :].astype(x.dtype), window_strides=(1,), padding=[(k_width - 1, 0)],
        dimension_numbers=("NWC", "WIO", "NWC"), feature_group_count=c)


def _rope(x, cos, sin):
    x1, x2 = jnp.split(x, 2, axis=-1)
    c = cos[None, :, None, :].astype(x.dtype)
    s = sin[None, :, None, :].astype(x.dtype)
    return jnp.concatenate([x1 * c - x2 * s, x1 * s + x2 * c], axis=-1)


def _short_conv_mixer(x, w_in, conv_w, w_out):
    b_gate, c_gate, h = jnp.split(x @ w_in, 3, axis=-1)
    return (b_gate * _causal_dwconv(c_gate * h, conv_w)) @ w_out


def _mla_mixer(x, w_dq, g_q, w_uq, w_dkv, g_kv, w_uk, w_uv, w_o):
    bsz, seq, _ = x.shape
    pos = jnp.arange(seq, dtype=jnp.float32)
    inv_freq = ROPE_THETA ** (-jnp.arange(0, QK_ROPE, 2, dtype=jnp.float32) / QK_ROPE)
    ang = pos[:, None] * inv_freq[None, :]
    cos, sin = jnp.cos(ang), jnp.sin(ang)
    cq = _rms_norm(x @ w_dq, g_q)
    q = (cq @ w_uq).reshape(bsz, seq, MLA_HEADS, QK_NOPE + QK_ROPE)
    q_nope, q_pe = q[..., :QK_NOPE], _rope(q[..., QK_NOPE:], cos, sin)
    ckv_full = x @ w_dkv
    ckv = _rms_norm(ckv_full[..., :KV_LORA], g_kv)
    k_pe = _rope(ckv_full[..., None, KV_LORA:], cos, sin)[:, :, 0, :]
    k_nope = jnp.einsum("bsc,chd->bshd", ckv, w_uk)
    v = jnp.einsum("bsc,chd->bshd", ckv, w_uv)
    scale = (QK_NOPE + QK_ROPE) ** -0.5
    n_blk = seq // Q_BLOCK
    qn_b = q_nope.reshape(bsz, n_blk, Q_BLOCK, MLA_HEADS, QK_NOPE).transpose(1, 0, 2, 3, 4)
    qp_b = q_pe.reshape(bsz, n_blk, Q_BLOCK, MLA_HEADS, QK_ROPE).transpose(1, 0, 2, 3, 4)
    k_chunk = jnp.arange(seq) // CHUNK

    def block(args):
        qn, qp, blk = args
        s = (jnp.einsum("bqhd,bkhd->bhqk", qn, k_nope)
             + jnp.einsum("bqhd,bkd->bhqk", qp, k_pe)).astype(jnp.float32) * scale
        q_chunk = (blk * Q_BLOCK + jnp.arange(Q_BLOCK)) // CHUNK
        allowed = k_chunk[None, :] <= q_chunk[:, None]
        p = jax.nn.softmax(jnp.where(allowed[None, None], s, -jnp.inf), axis=-1).astype(v.dtype)
        return jnp.einsum("bhqk,bkhd->bqhd", p, v)

    o = lax.map(block, (qn_b, qp_b, jnp.arange(n_blk)))
    o = o.transpose(1, 0, 2, 3, 4).reshape(bsz, seq, MLA_HEADS * V_HEAD)
    return o @ w_o


def _conformer_conv_mixer(x, w_pw1, b_pw1, dw_w, dw_b, norm_g, norm_b, w_pw2, b_pw2):
    a, gate = jnp.split(x @ w_pw1 + b_pw1, 2, axis=-1)
    h = a * jax.nn.sigmoid(gate)
    h = _causal_dwconv(h, dw_w) + dw_b
    h = jax.nn.silu(_layer_norm(h, norm_g, norm_b))
    return h @ w_pw2 + b_pw2


def _sq_relu_mlp(x, w1, w2):
    return jnp.square(jax.nn.relu(x @ w1)) @ w2


def _normal(k, shape, fan_in, scale=1.0):
    return jax.random.normal(k, shape, jnp.float32) * (scale * fan_in ** -0.5)


def setup_inputs(seed: int = 0) -> dict:
    key = jax.random.key(seed)
    ks = iter(jax.random.split(key, 40))
    D = D_MODEL
    gain = lambda shape: 1.0 + 0.02 * jax.random.normal(next(ks), shape, jnp.float32)
    bias = lambda shape: 0.02 * jax.random.normal(next(ks), shape, jnp.float32)
    return {
        "x": jax.random.normal(next(ks), (BATCH, SEQ, D), jnp.float32),
        "sc_w_in": _normal(next(ks), (N_A, D, 3 * D), D),
        "sc_conv_w": _normal(next(ks), (N_A, SC_WIDTH, D), SC_WIDTH),
        "sc_w_out": _normal(next(ks), (N_A, D, D), D, BETA),
        "mla_w_dq": _normal(next(ks), (N_B, D, Q_LORA), D),
        "mla_g_q": gain((N_B, Q_LORA)),
        "mla_w_uq": _normal(next(ks), (N_B, Q_LORA, MLA_HEADS * (QK_NOPE + QK_ROPE)), Q_LORA),
        "mla_w_dkv": _normal(next(ks), (N_B, D, KV_LORA + QK_ROPE), D),
        "mla_g_kv": gain((N_B, KV_LORA)),
        "mla_w_uk": _normal(next(ks), (N_B, KV_LORA, MLA_HEADS, QK_NOPE), KV_LORA),
        "mla_w_uv": _normal(next(ks), (N_B, KV_LORA, MLA_HEADS, V_HEAD), KV_LORA, BETA),
        "mla_w_o": _normal(next(ks), (N_B, MLA_HEADS * V_HEAD, D), MLA_HEADS * V_HEAD, BETA),
        "cf_w_pw1": _normal(next(ks), (N_C, D, 2 * D), D),
        "cf_b_pw1": bias((N_C, 2 * D)),
        "cf_dw_w": _normal(next(ks), (N_C, CONF_WIDTH, D), CONF_WIDTH),
        "cf_dw_b": bias((N_C, D)),
        "cf_norm_g": gain((N_C, D)),
        "cf_norm_b": bias((N_C, D)),
        "cf_w_pw2": _normal(next(ks), (N_C, D, D), D, BETA),
        "cf_b_pw2": bias((N_C, D)),
        "ff_w1": _normal(next(ks), (DEPTH, D, D_FF), D, BETA),
        "ff_w2": _normal(next(ks), (DEPTH, D_FF, D), D_FF, BETA),
        "ln_mix_g": gain((DEPTH, D)),
        "ln_mix_b": bias((DEPTH, D)),
        "ln_ff_g": gain((DEPTH, D)),
        "ln_ff_b": bias((DEPTH, D)),
    }


def reference(x, sc_w_in, sc_conv_w, sc_w_out,
              mla_w_dq, mla_g_q, mla_w_uq, mla_w_dkv, mla_g_kv, mla_w_uk, mla_w_uv, mla_w_o,
              cf_w_pw1, cf_b_pw1, cf_dw_w, cf_dw_b, cf_norm_g, cf_norm_b, cf_w_pw2, cf_b_pw2,
              ff_w1, ff_w2, ln_mix_g, ln_mix_b, ln_ff_g, ln_ff_b):
    for i in range(DEPTH):
        m, j = i % N_MIXERS, i // N_MIXERS
        if m == 0:
            y = _short_conv_mixer(x, sc_w_in[j], sc_conv_w[j], sc_w_out[j])
        elif m == 1:
            y = _mla_mixer(x, mla_w_dq[j], mla_g_q[j], mla_w_uq[j], mla_w_dkv[j], mla_g_kv[j],
                           mla_w_uk[j], mla_w_uv[j], mla_w_o[j])
        else:
            y = _conformer_conv_mixer(x, cf_w_pw1[j], cf_b_pw1[j], cf_dw_w[j], cf_dw_b[j],
                                      cf_norm_g[j], cf_norm_b[j], cf_w_pw2[j], cf_b_pw2[j])
        x = _layer_norm(ALPHA * x + y, ln_mix_g[i], ln_mix_b[i])
        x = _layer_norm(ALPHA * x + _sq_relu_mlp(x, ff_w1[i], ff_w2[i]), ln_ff_g[i], ln_ff_b[i])
    return x
```

```python
import functools

import jax
import jax.numpy as jnp
from jax import lax
from jax.experimental import pallas as pl
from jax.experimental.pallas import tpu as pltpu

D_MODEL = 1024
DEPTH = 4
CHUNK = 64
N_MIXERS = 3
ALPHA = (2.0 * DEPTH) ** 0.25
LN_EPS = 1e-5
RMS_EPS = 1e-6
SC_WIDTH = 3
MLA_HEADS = 8
QK_NOPE = 128
QK_ROPE = 64
V_HEAD = 128
Q_LORA = 3 * D_MODEL // 8
KV_LORA = D_MODEL // 4
ROPE_THETA = 10000.0
CONF_WIDTH = 31
D_FF = 4 * D_MODEL

LANES = 128
SUBLANES = 8
N_SLABS = D_MODEL // LANES
HEAD_W = 2 * LANES
VMEM_LIMIT = 56 * 1024 * 1024

ROW_TILE = 512
ATT_TQ = 256
ATT_TK = 256
SC_HALO = 8
CF_HALO = 32
CONV_ROWS = 128
FF_CHUNKS = 4

BF16 = jnp.bfloat16
F32 = jnp.float32


def _dot(a, b):
    return jnp.dot(a, b, preferred_element_type=F32)


def _layer_norm(z, g, b):
    mu = jnp.mean(z, axis=-1, keepdims=True)
    zc = z - mu
    var = jnp.mean(zc * zc, axis=-1, keepdims=True)
    return zc * lax.rsqrt(var + LN_EPS) * g + b


def _rms_norm(z, g):
    return z * lax.rsqrt(jnp.mean(z * z, axis=-1, keepdims=True) + RMS_EPS) * g


def _const_spec(shape):
    nd = len(shape)
    return pl.BlockSpec(shape, lambda *_: (0,) * nd, pipeline_mode=pl.Buffered(1))


def _row_spec(width):
    return pl.BlockSpec((ROW_TILE, width), lambda i: (i, 0))


def _params(*sem):
    return pltpu.CompilerParams(dimension_semantics=sem, vmem_limit_bytes=VMEM_LIMIT)


def _mlp_ln_kernel(x_ref, w1_ref, w2_ref, g_ref, b_ref, o_ref):
    x = x_ref[...]
    xb = x.astype(BF16)
    fc = D_FF // FF_CHUNKS
    y = None
    for c in range(FF_CHUNKS):
        h = _dot(xb, w1_ref[:, c * fc:(c + 1) * fc])
        h = jnp.maximum(h, 0.0)
        h = (h * h).astype(BF16)
        yc = _dot(h, w2_ref[c * fc:(c + 1) * fc, :])
        y = yc if y is None else y + yc
    o_ref[...] = _layer_norm(ALPHA * x + y, g_ref[...], b_ref[...])


def _mlp_ln(x, w1, w2, g, b):
    n = x.shape[0]
    return pl.pallas_call(
        _mlp_ln_kernel,
        grid=(n // ROW_TILE,),
        in_specs=[_row_spec(D_MODEL), _const_spec((D_MODEL, D_FF)), _const_spec((D_FF, D_MODEL)),
                  _const_spec((1, D_MODEL)), _const_spec((1, D_MODEL))],
        out_specs=_row_spec(D_MODEL),
        out_shape=jax.ShapeDtypeStruct((n, D_MODEL), F32),
        compiler_params=_params("parallel"),
        name="mlp_ln",
    )(x, w1, w2, g, b)


def _sconv_kernel(x_ref, win_ref, cw_ref, wout_ref, g_ref, b_ref, o_ref, ubuf, gbuf, *, tiles_per_seq):
    @pl.when(pl.program_id(0) % tiles_per_seq == 0)
    def _():
        ubuf[:, 0:SC_HALO, :] = jnp.zeros((N_SLABS, SC_HALO, LANES), F32)

    x = x_ref[...]
    xb = x.astype(BF16)
    b_gate = _dot(xb, win_ref[:, 0:D_MODEL])
    c_gate = _dot(xb, win_ref[:, D_MODEL:2 * D_MODEL])
    h = _dot(xb, win_ref[:, 2 * D_MODEL:3 * D_MODEL])
    u = c_gate * h
    cw = cw_ref[...]
    for s in range(N_SLABS):
        cols = slice(s * LANES, (s + 1) * LANES)
        ubuf[s, SC_HALO:SC_HALO + ROW_TILE, :] = u[:, cols]
        conv = cw[SC_WIDTH - 1:SC_WIDTH, cols] * u[:, cols]
        for k in range(SC_WIDTH - 1):
            shift = SC_WIDTH - 1 - k
            conv = conv + cw[k:k + 1, cols] * ubuf[s, SC_HALO - shift:SC_HALO - shift + ROW_TILE, :]
        ubuf[s, 0:SC_HALO, :] = ubuf[s, ROW_TILE:ROW_TILE + SC_HALO, :]
        gbuf[:, cols] = (b_gate[:, cols] * conv).astype(BF16)
    y = _dot(gbuf[...], wout_ref[...])
    o_ref[...] = _layer_norm(ALPHA * x + y, g_ref[...], b_ref[...])


def _sconv_ln(x, w_in, conv_w, w_out, g, b, seq):
    n = x.shape[0]
    return pl.pallas_call(
        functools.partial(_sconv_kernel, tiles_per_seq=seq // ROW_TILE),
        grid=(n // ROW_TILE,),
        in_specs=[_row_spec(D_MODEL), _const_spec((D_MODEL, 3 * D_MODEL)), _const_spec((SC_WIDTH, D_MODEL)),
                  _const_spec((D_MODEL, D_MODEL)), _const_spec((1, D_MODEL)), _const_spec((1, D_MODEL))],
        out_specs=_row_spec(D_MODEL),
        out_shape=jax.ShapeDtypeStruct((n, D_MODEL), F32),
        scratch_shapes=[pltpu.VMEM((N_SLABS, SC_HALO + ROW_TILE, LANES), F32),
                        pltpu.VMEM((ROW_TILE, D_MODEL), BF16)],
        compiler_params=_params("arbitrary"),
        name="sconv_ln",
    )(x, w_in, conv_w, w_out, g, b)


def _conf_kernel(x_ref, w1_ref, b1_ref, dww_ref, dwb_ref, ng_ref, nb_ref, w2_ref, b2_ref, g_ref, b_ref,
                 o_ref, hbuf, cbuf, *, tiles_per_seq):
    @pl.when(pl.program_id(0) % tiles_per_seq == 0)
    def _():
        hbuf[:, 0:CF_HALO, :] = jnp.zeros((N_SLABS, CF_HALO, LANES), F32)

    x = x_ref[...]
    xb = x.astype(BF16)
    a = _dot(xb, w1_ref[:, 0:D_MODEL]) + b1_ref[:, 0:D_MODEL]
    gate = _dot(xb, w1_ref[:, D_MODEL:2 * D_MODEL]) + b1_ref[:, D_MODEL:2 * D_MODEL]
    h = a * jax.nn.sigmoid(gate)
    for s in range(N_SLABS):
        cols = slice(s * LANES, (s + 1) * LANES)
        hbuf[s, CF_HALO:CF_HALO + ROW_TILE, :] = h[:, cols]
        for r0 in range(0, ROW_TILE, CONV_ROWS):
            acc = jnp.zeros((CONV_ROWS, LANES), F32) + dwb_ref[:, cols]
            for k in range(CONF_WIDTH):
                start = CF_HALO + r0 - (CONF_WIDTH - 1) + k
                acc = acc + dww_ref[k:k + 1, cols] * hbuf[s, start:start + CONV_ROWS, :]
            cbuf[r0:r0 + CONV_ROWS, cols] = acc
        hbuf[s, 0:CF_HALO, :] = hbuf[s, ROW_TILE:ROW_TILE + CF_HALO, :]
    hn = _layer_norm(cbuf[...], ng_ref[...], nb_ref[...])
    hs = (hn * jax.nn.sigmoid(hn)).astype(BF16)
    y = _dot(hs, w2_ref[...]) + b2_ref[...]
    o_ref[...] = _layer_norm(ALPHA * x + y, g_ref[...], b_ref[...])


def _conf_ln(x, w_pw1, b_pw1, dw_w, dw_b, norm_g, norm_b, w_pw2, b_pw2, g, b, seq):
    n = x.shape[0]
    vec = _const_spec((1, D_MODEL))
    return pl.pallas_call(
        functools.partial(_conf_kernel, tiles_per_seq=seq // ROW_TILE),
        grid=(n // ROW_TILE,),
        in_specs=[_row_spec(D_MODEL), _const_spec((D_MODEL, 2 * D_MODEL)), _const_spec((1, 2 * D_MODEL)),
                  _const_spec((CONF_WIDTH, D_MODEL)), vec, vec, vec,
                  _const_spec((D_MODEL, D_MODEL)), vec, vec, vec],
        out_specs=_row_spec(D_MODEL),
        out_shape=jax.ShapeDtypeStruct((n, D_MODEL), F32),
        scratch_shapes=[pltpu.VMEM((N_SLABS, CF_HALO + ROW_TILE, LANES), F32),
                        pltpu.VMEM((ROW_TILE, D_MODEL), F32)],
        compiler_params=_params("arbitrary"),
        name="conf_ln",
    )(x, w_pw1, b_pw1, dw_w, dw_b, norm_g, norm_b, w_pw2, b_pw2, g, b)


def _mla_proj_kernel(x_ref, cos_ref, sin_ref, wdq_ref, gq_ref, wuq_ref, wdkv_ref, gkv_ref, wkpe_ref,
                     wuk_ref, wuvt_ref, q_ref, k_ref, vt_ref):
    nh = MLA_HEADS * LANES
    scale = (QK_NOPE + QK_ROPE) ** -0.5
    xb = x_ref[...].astype(BF16)
    cos = cos_ref[...]
    sin = sin_ref[...]

    cq = _rms_norm(_dot(xb, wdq_ref[...]), gq_ref[...]).astype(BF16)
    qall = _dot(cq, wuq_ref[...])
    for hd in range(MLA_HEADS):
        cols = slice(hd * LANES, (hd + 1) * LANES)
        q_ref[:, hd * HEAD_W:hd * HEAD_W + LANES] = (qall[:, cols] * scale).astype(BF16)
        pe = qall[:, nh + hd * LANES:nh + (hd + 1) * LANES]
        pe_rot = qall[:, 2 * nh + hd * LANES:2 * nh + (hd + 1) * LANES]
        q_ref[:, hd * HEAD_W + LANES:(hd + 1) * HEAD_W] = ((pe * cos + pe_rot * sin) * scale).astype(BF16)

    ckv = _rms_norm(_dot(xb, wdkv_ref[...]), gkv_ref[...]).astype(BF16)
    kpe2 = _dot(xb, wkpe_ref[...])
    k_pe = (kpe2[:, 0:LANES] * cos + kpe2[:, LANES:2 * LANES] * sin).astype(BF16)
    k_nope = _dot(ckv, wuk_ref[...])
    for hd in range(MLA_HEADS):
        k_ref[:, hd * HEAD_W:hd * HEAD_W + LANES] = k_nope[:, hd * LANES:(hd + 1) * LANES].astype(BF16)
        k_ref[:, hd * HEAD_W + LANES:(hd + 1) * HEAD_W] = k_pe
    vt = lax.dot_general(wuvt_ref[...], ckv, (((1,), (1,)), ((), ())), preferred_element_type=F32)
    vt_ref[0] = vt.astype(BF16)


def _mla_proj(x, cos, sin, wdq, gq, wuq, wdkv, gkv, wkpe, wuk, wuvt, batch, seq):
    n = x.shape[0]
    tps = seq // ROW_TILE
    hw = MLA_HEADS * HEAD_W
    hv = MLA_HEADS * V_HEAD
    pos_spec = pl.BlockSpec((ROW_TILE, LANES), lambda i: (i % tps, 0))
    return pl.pallas_call(
        _mla_proj_kernel,
        grid=(n // ROW_TILE,),
        in_specs=[_row_spec(D_MODEL), pos_spec, pos_spec,
                  _const_spec((D_MODEL, Q_LORA)), _const_spec((1, Q_LORA)),
                  _const_spec((Q_LORA, 3 * MLA_HEADS * LANES)),
                  _const_spec((D_MODEL, KV_LORA)), _const_spec((1, KV_LORA)),
                  _const_spec((D_MODEL, 2 * LANES)),
                  _const_spec((KV_LORA, MLA_HEADS * QK_NOPE)), _const_spec((hv, KV_LORA))],
        out_specs=[_row_spec(hw), _row_spec(hw),
                   pl.BlockSpec((1, hv, ROW_TILE), lambda i: (i // tps, 0, i % tps))],
        out_shape=[jax.ShapeDtypeStruct((n, hw), BF16), jax.ShapeDtypeStruct((n, hw), BF16),
                   jax.ShapeDtypeStruct((batch, hv, seq), BF16)],
        compiler_params=_params("parallel"),
        name="mla_proj",
    )(x, cos, sin, wdq, gq, wuq, wdkv, gkv, wkpe, wuk, wuvt)


def _mla_attn_kernel(q_ref, k_ref, vt_ref, o_ref, *, seq):
    rows = lax.broadcasted_iota(jnp.int32, (ATT_TK, ATT_TQ), 0) // CHUNK
    colsq = lax.broadcasted_iota(jnp.int32, (ATT_TK, ATT_TQ), 1) // CHUNK
    diag_allowed = rows <= colsq
    for qi in range(seq // ATT_TQ):
        q = q_ref[0, qi * ATT_TQ:(qi + 1) * ATT_TQ, :]
        m = jnp.full((1, ATT_TQ), -jnp.inf, F32)
        l = jnp.zeros((1, ATT_TQ), F32)
        acc = jnp.zeros((V_HEAD, ATT_TQ), F32)
        n_k = (qi + 1) * ATT_TQ // ATT_TK
        for kj in range(n_k):
            k = k_ref[0, kj * ATT_TK:(kj + 1) * ATT_TK, :]
            s = lax.dot_general(k, q, (((1,), (1,)), ((), ())), preferred_element_type=F32)
            if kj == n_k - 1:
                s = jnp.where(diag_allowed, s, -1e30)
            m_new = jnp.maximum(m, jnp.max(s, axis=0, keepdims=True))
            p = jnp.exp(s - m_new)
            alpha = jnp.exp(m - m_new)
            l = alpha * l + jnp.sum(p, axis=0, keepdims=True)
            pv = _dot(vt_ref[0, :, kj * ATT_TK:(kj + 1) * ATT_TK], p.astype(BF16))
            acc = alpha * acc + pv
            m = m_new
        o_ref[0, :, qi * ATT_TQ:(qi + 1) * ATT_TQ] = (acc / l).astype(BF16)


def _mla_attn(q, k, vt, batch, seq):
    assert ATT_TQ == ATT_TK and ATT_TQ % CHUNK == 0 and seq % ATT_TQ == 0
    qk_spec = pl.BlockSpec((1, seq, HEAD_W), lambda b, h: (b, 0, h))
    vt_spec = pl.BlockSpec((1, V_HEAD, seq), lambda b, h: (b, h, 0))
    return pl.pallas_call(
        functools.partial(_mla_attn_kernel, seq=seq),
        grid=(batch, MLA_HEADS),
        in_specs=[qk_spec, qk_spec, vt_spec],
        out_specs=vt_spec,
        out_shape=jax.ShapeDtypeStruct((batch, MLA_HEADS * V_HEAD, seq), BF16),
        compiler_params=_params("parallel", "parallel"),
        name="mla_attn",
    )(q.reshape(batch, seq, MLA_HEADS * HEAD_W), k.reshape(batch, seq, MLA_HEADS * HEAD_W), vt)


def _mla_out_kernel(x_ref, ot_ref, wo_ref, g_ref, b_ref, o_ref):
    y = lax.dot_general(ot_ref[0], wo_ref[...], (((0,), (0,)), ((), ())), preferred_element_type=F32)
    o_ref[...] = _layer_norm(ALPHA * x_ref[...] + y, g_ref[...], b_ref[...])


def _mla_out_ln(x, ot, w_o, g, b, seq):
    n = x.shape[0]
    tps = seq // ROW_TILE
    hv = MLA_HEADS * V_HEAD
    return pl.pallas_call(
        _mla_out_kernel,
        grid=(n // ROW_TILE,),
        in_specs=[_row_spec(D_MODEL), pl.BlockSpec((1, hv, ROW_TILE), lambda i: (i // tps, 0, i % tps)),
                  _const_spec((hv, D_MODEL)), _const_spec((1, D_MODEL)), _const_spec((1, D_MODEL))],
        out_specs=_row_spec(D_MODEL),
        out_shape=jax.ShapeDtypeStruct((n, D_MODEL), F32),
        compiler_params=_params("parallel"),
        name="mla_out_ln",
    )(x, ot, w_o, g, b)


def _rope_tables(seq):
    pos = jnp.arange(seq, dtype=F32)
    inv_freq = ROPE_THETA ** (-jnp.arange(0, QK_ROPE, 2, dtype=F32) / QK_ROPE)
    ang = pos[:, None] * inv_freq[None, :]
    pad = jnp.zeros((seq, LANES - QK_ROPE), F32)
    cos, sin = jnp.cos(ang), jnp.sin(ang)
    return jnp.concatenate([cos, cos, pad], axis=1), jnp.concatenate([sin, sin, pad], axis=1)


def _rotate_half_cols(w):
    half = w.shape[-1] // 2
    return jnp.concatenate([-w[..., half:], w[..., :half]], axis=-1)


def _pad_lanes(w):
    return jnp.pad(w, [(0, 0)] * (w.ndim - 1) + [(0, LANES - w.shape[-1])])


def _mla_layer(x, w_dq, g_q, w_uq, w_dkv, g_kv, w_uk, w_uv, w_o, g, b, batch, seq):
    wq = w_uq.reshape(Q_LORA, MLA_HEADS, QK_NOPE + QK_ROPE)
    wq_pe = wq[..., QK_NOPE:]
    wuq = jnp.concatenate(
        [wq[..., :QK_NOPE].reshape(Q_LORA, -1),
         _pad_lanes(wq_pe).reshape(Q_LORA, -1),
         _pad_lanes(_rotate_half_cols(wq_pe)).reshape(Q_LORA, -1)], axis=1).astype(BF16)
    wk_pe = w_dkv[:, KV_LORA:]
    wkpe = jnp.concatenate([_pad_lanes(wk_pe), _pad_lanes(_rotate_half_cols(wk_pe))], axis=1).astype(BF16)
    wuvt = w_uv.reshape(KV_LORA, MLA_HEADS * V_HEAD).T.astype(BF16)
    cos, sin = _rope_tables(seq)
    q, k, vt = _mla_proj(x, cos, sin, w_dq.astype(BF16), g_q[None], wuq, w_dkv[:, :KV_LORA].astype(BF16),
                         g_kv[None], wkpe, w_uk.reshape(KV_LORA, -1).astype(BF16), wuvt, batch, seq)
    ot = _mla_attn(q, k, vt, batch, seq)
    return _mla_out_ln(x, ot, w_o.astype(BF16), g, b, seq)


def kernel(x, sc_w_in, sc_conv_w, sc_w_out, mla_w_dq, mla_g_q, mla_w_uq, mla_w_dkv, mla_g_kv, mla_w_uk, mla_w_uv, mla_w_o, cf_w_pw1, cf_b_pw1, cf_dw_w, cf_dw_b, cf_norm_g, cf_norm_b, cf_w_pw2, cf_b_pw2, ff_w1, ff_w2, ln_mix_g, ln_mix_b, ln_ff_g, ln_ff_b):
    batch, seq, d = x.shape
    assert d == D_MODEL and seq % ROW_TILE == 0
    h = x.reshape(batch * seq, d)
    for i in range(DEPTH):
        m, j = i % N_MIXERS, i // N_MIXERS
        g, b = ln_mix_g[i][None], ln_mix_b[i][None]
        if m == 0:
            h = _sconv_ln(h, sc_w_in[j].astype(BF16), sc_conv_w[j], sc_w_out[j].astype(BF16), g, b, seq)
        elif m == 1:
            h = _mla_layer(h, mla_w_dq[j], mla_g_q[j], mla_w_uq[j], mla_w_dkv[j], mla_g_kv[j],
                           mla_w_uk[j], mla_w_uv[j], mla_w_o[j], g, b, batch, seq)
        else:
            h = _conf_ln(h, cf_w_pw1[j].astype(BF16), cf_b_pw1[j][None], cf_dw_w[j], cf_dw_b[j][None],
                         cf_norm_g[j][None], cf_norm_b[j][None], cf_w_pw2[j].astype(BF16),
                         cf_b_pw2[j][None], g, b, seq)
        h = _mlp_ln(h, ff_w1[i].astype(BF16), ff_w2[i].astype(BF16), ln_ff_g[i][None], ln_ff_b[i][None])
    return h.reshape(batch, seq, d)
```

```python
import functools

import jax
import jax.numpy as jnp
from jax import lax
from jax.experimental import pallas as pl
from jax.experimental.pallas import tpu as pltpu

D_MODEL = 1024
DEPTH = 4
CHUNK = 64
N_MIXERS = 3
ALPHA = (2.0 * DEPTH) ** 0.25
LN_EPS = 1e-5
RMS_EPS = 1e-6
SC_WIDTH = 3
MLA_HEADS = 8
QK_NOPE = 128
QK_ROPE = 64
V_HEAD = 128
Q_LORA = 3 * D_MODEL // 8
KV_LORA = D_MODEL // 4
ROPE_THETA = 10000.0
CONF_WIDTH = 31
D_FF = 4 * D_MODEL

LANES = 128
SUBLANES = 8
N_SLABS = D_MODEL // LANES
HEAD_W = 2 * LANES
VMEM_LIMIT = 56 * 1024 * 1024

ROW_TILE = 512
ATT_TQ = 256
MLP_SUBTILES = 2
SC_HALO = 8
CF_HALO = 32
CONV_ROWS = 128
FF_CHUNKS = 4

BF16 = jnp.bfloat16
F32 = jnp.float32
LOG2_E = 1.4426950408889634


def _dot(a, b):
    return jnp.dot(a, b, preferred_element_type=F32)


def _layer_norm(z, g, b):
    mu = jnp.mean(z, axis=-1, keepdims=True)
    zc = z - mu
    var = jnp.mean(zc * zc, axis=-1, keepdims=True)
    return zc * lax.rsqrt(var + LN_EPS) * g + b


def _rms_norm(z, g):
    return z * lax.rsqrt(jnp.mean(z * z, axis=-1, keepdims=True) + RMS_EPS) * g


def _const_spec(shape):
    nd = len(shape)
    return pl.BlockSpec(shape, lambda *_: (0,) * nd, pipeline_mode=pl.Buffered(1))


def _row_spec(width):
    return pl.BlockSpec((ROW_TILE, width), lambda i: (i, 0))


def _params(*sem):
    return pltpu.CompilerParams(dimension_semantics=sem, vmem_limit_bytes=VMEM_LIMIT)


def _cast_specs(jobs, n_steps):
    in_specs, out_specs, shapes = [], [], []
    for w, layer in jobs:
        _, rows, cols = w.shape
        assert rows % n_steps == 0 and (rows // n_steps) % (2 * SUBLANES) == 0
        in_specs.append(pl.BlockSpec((None, rows // n_steps, cols), lambda i, layer=layer: (layer, i, 0)))
        out_specs.append(pl.BlockSpec((rows // n_steps, cols), lambda i: (i, 0)))
        shapes.append(jax.ShapeDtypeStruct((rows, cols), BF16))
    return in_specs, out_specs, shapes


def _cast_blocks(src_refs, dst_refs):
    for src, dst in zip(src_refs, dst_refs):
        dst[...] = src[...].astype(BF16)


def _mlp_ln_kernel(x_ref, w1_ref, w2_ref, g_ref, b_ref, *refs, n_cast):
    cast_src, o_ref, cast_dst = refs[:n_cast], refs[n_cast], refs[n_cast + 1:]
    _cast_blocks(cast_src, cast_dst)
    fc = D_FF // FF_CHUNKS
    for t in range(MLP_SUBTILES):
        rows = slice(t * ROW_TILE, (t + 1) * ROW_TILE)
        x = x_ref[rows, :]
        xb = x.astype(BF16)
        y = None
        for c in range(FF_CHUNKS):
            h = _dot(xb, w1_ref[:, c * fc:(c + 1) * fc])
            h = jnp.maximum(h, 0.0)
            h = (h * h).astype(BF16)
            yc = _dot(h, w2_ref[c * fc:(c + 1) * fc, :])
            y = yc if y is None else y + yc
        o_ref[rows, :] = _layer_norm(ALPHA * x + y, g_ref[...], b_ref[...])


def _mlp_ln(x, w1, w2, g, b, cast=()):
    n = x.shape[0]
    step_rows = MLP_SUBTILES * ROW_TILE
    n_steps = n // step_rows
    row_spec = pl.BlockSpec((step_rows, D_MODEL), lambda i: (i, 0))
    cast_in, cast_out, cast_shapes = _cast_specs(cast, n_steps)
    return pl.pallas_call(
        functools.partial(_mlp_ln_kernel, n_cast=len(cast)),
        grid=(n_steps,),
        in_specs=[row_spec, _const_spec((D_MODEL, D_FF)), _const_spec((D_FF, D_MODEL)),
                  _const_spec((1, D_MODEL)), _const_spec((1, D_MODEL))] + cast_in,
        out_specs=[row_spec] + cast_out,
        out_shape=[jax.ShapeDtypeStruct((n, D_MODEL), F32)] + cast_shapes,
        compiler_params=_params("parallel"),
        name="mlp_ln",
    )(x, w1, w2, g, b, *[w for w, _ in cast])


def _sconv_kernel(x_ref, win_ref, cw_ref, wout_ref, g_ref, b_ref, *refs, tiles_per_seq, n_cast):
    cast_src, o_ref, cast_dst = refs[:n_cast], refs[n_cast], refs[n_cast + 1:2 * n_cast + 1]
    ubuf, gbuf = refs[2 * n_cast + 1:]
    _cast_blocks(cast_src, cast_dst)

    @pl.when(pl.program_id(0) % tiles_per_seq == 0)
    def _():
        ubuf[:, 0:SC_HALO, :] = jnp.zeros((N_SLABS, SC_HALO, LANES), F32)

    x = x_ref[...]
    xb = x.astype(BF16)
    b_gate = _dot(xb, win_ref[:, 0:D_MODEL])
    c_gate = _dot(xb, win_ref[:, D_MODEL:2 * D_MODEL])
    h = _dot(xb, win_ref[:, 2 * D_MODEL:3 * D_MODEL])
    u = c_gate * h
    cw = cw_ref[...]
    for s in range(N_SLABS):
        cols = slice(s * LANES, (s + 1) * LANES)
        ubuf[s, SC_HALO:SC_HALO + ROW_TILE, :] = u[:, cols]
        conv = cw[SC_WIDTH - 1:SC_WIDTH, cols] * u[:, cols]
        for k in range(SC_WIDTH - 1):
            shift = SC_WIDTH - 1 - k
            conv = conv + cw[k:k + 1, cols] * ubuf[s, SC_HALO - shift:SC_HALO - shift + ROW_TILE, :]
        ubuf[s, 0:SC_HALO, :] = ubuf[s, ROW_TILE:ROW_TILE + SC_HALO, :]
        gbuf[:, cols] = (b_gate[:, cols] * conv).astype(BF16)
    y = _dot(gbuf[...], wout_ref[...])
    o_ref[...] = _layer_norm(ALPHA * x + y, g_ref[...], b_ref[...])


def _sconv_ln(x, w_in, conv_w, w_out, g, b, seq, cast=()):
    n = x.shape[0]
    n_steps = n // ROW_TILE
    cast_in, cast_out, cast_shapes = _cast_specs(cast, n_steps)
    return pl.pallas_call(
        functools.partial(_sconv_kernel, tiles_per_seq=seq // ROW_TILE, n_cast=len(cast)),
        grid=(n_steps,),
        in_specs=[_row_spec(D_MODEL), _const_spec((D_MODEL, 3 * D_MODEL)), _const_spec((SC_WIDTH, D_MODEL)),
                  _const_spec((D_MODEL, D_MODEL)), _const_spec((1, D_MODEL)), _const_spec((1, D_MODEL))]
                 + cast_in,
        out_specs=[_row_spec(D_MODEL)] + cast_out,
        out_shape=[jax.ShapeDtypeStruct((n, D_MODEL), F32)] + cast_shapes,
        scratch_shapes=[pltpu.VMEM((N_SLABS, SC_HALO + ROW_TILE, LANES), F32),
                        pltpu.VMEM((ROW_TILE, D_MODEL), BF16)],
        compiler_params=_params("arbitrary"),
        name="sconv_ln",
    )(x, w_in, conv_w, w_out, g, b, *[w for w, _ in cast])


def _conf_kernel(x_ref, w1_ref, b1_ref, dww_ref, dwb_ref, ng_ref, nb_ref, w2_ref, b2_ref, g_ref, b_ref,
                 o_ref, hbuf, cbuf, *, tiles_per_seq):
    @pl.when(pl.program_id(0) % tiles_per_seq == 0)
    def _():
        hbuf[:, 0:CF_HALO, :] = jnp.zeros((N_SLABS, CF_HALO, LANES), F32)

    x = x_ref[...]
    xb = x.astype(BF16)
    a = _dot(xb, w1_ref[:, 0:D_MODEL]) + b1_ref[:, 0:D_MODEL]
    gate = _dot(xb, w1_ref[:, D_MODEL:2 * D_MODEL]) + b1_ref[:, D_MODEL:2 * D_MODEL]
    h = a * jax.nn.sigmoid(gate)
    for s in range(N_SLABS):
        cols = slice(s * LANES, (s + 1) * LANES)
        hbuf[s, CF_HALO:CF_HALO + ROW_TILE, :] = h[:, cols]
        for r0 in range(0, ROW_TILE, CONV_ROWS):
            acc = jnp.zeros((CONV_ROWS, LANES), F32) + dwb_ref[:, cols]
            for k in range(CONF_WIDTH):
                start = CF_HALO + r0 - (CONF_WIDTH - 1) + k
                acc = acc + dww_ref[k:k + 1, cols] * hbuf[s, start:start + CONV_ROWS, :]
            cbuf[r0:r0 + CONV_ROWS, cols] = acc
        hbuf[s, 0:CF_HALO, :] = hbuf[s, ROW_TILE:ROW_TILE + CF_HALO, :]
    hn = _layer_norm(cbuf[...], ng_ref[...], nb_ref[...])
    hs = (hn * jax.nn.sigmoid(hn)).astype(BF16)
    y = _dot(hs, w2_ref[...]) + b2_ref[...]
    o_ref[...] = _layer_norm(ALPHA * x + y, g_ref[...], b_ref[...])


def _conf_ln(x, w_pw1, b_pw1, dw_w, dw_b, norm_g, norm_b, w_pw2, b_pw2, g, b, seq):
    n = x.shape[0]
    vec = _const_spec((1, D_MODEL))
    return pl.pallas_call(
        functools.partial(_conf_kernel, tiles_per_seq=seq // ROW_TILE),
        grid=(n // ROW_TILE,),
        in_specs=[_row_spec(D_MODEL), _const_spec((D_MODEL, 2 * D_MODEL)), _const_spec((1, 2 * D_MODEL)),
                  _const_spec((CONF_WIDTH, D_MODEL)), vec, vec, vec,
                  _const_spec((D_MODEL, D_MODEL)), vec, vec, vec],
        out_specs=_row_spec(D_MODEL),
        out_shape=jax.ShapeDtypeStruct((n, D_MODEL), F32),
        scratch_shapes=[pltpu.VMEM((N_SLABS, CF_HALO + ROW_TILE, LANES), F32),
                        pltpu.VMEM((ROW_TILE, D_MODEL), F32)],
        compiler_params=_params("arbitrary"),
        name="conf_ln",
    )(x, w_pw1, b_pw1, dw_w, dw_b, norm_g, norm_b, w_pw2, b_pw2, g, b)


def _mla_proj_kernel(x_ref, cos_ref, sin_ref, wdq_ref, gq_ref, wuq_ref, wdkv_ref, gkv_ref, wkpe_ref,
                     wuk_ref, wuvt_ref, q_ref, k_ref, vt_ref):
    nh = MLA_HEADS * LANES
    scale = (QK_NOPE + QK_ROPE) ** -0.5 * LOG2_E
    xb = x_ref[...].astype(BF16)
    cos = cos_ref[...]
    sin = sin_ref[...]

    cq = _rms_norm(_dot(xb, wdq_ref[...]), gq_ref[...]).astype(BF16)
    qall = _dot(cq, wuq_ref[...])
    for hd in range(MLA_HEADS):
        cols = slice(hd * LANES, (hd + 1) * LANES)
        q_ref[:, hd * HEAD_W:hd * HEAD_W + LANES] = (qall[:, cols] * scale).astype(BF16)
        pe = qall[:, nh + hd * LANES:nh + (hd + 1) * LANES]
        pe_rot = qall[:, 2 * nh + hd * LANES:2 * nh + (hd + 1) * LANES]
        q_ref[:, hd * HEAD_W + LANES:(hd + 1) * HEAD_W] = ((pe * cos + pe_rot * sin) * scale).astype(BF16)

    ckv = _rms_norm(_dot(xb, wdkv_ref[...]), gkv_ref[...]).astype(BF16)
    kpe2 = _dot(xb, wkpe_ref[...])
    k_pe = (kpe2[:, 0:LANES] * cos + kpe2[:, LANES:2 * LANES] * sin).astype(BF16)
    k_nope = _dot(ckv, wuk_ref[...])
    for hd in range(MLA_HEADS):
        k_ref[:, hd * HEAD_W:hd * HEAD_W + LANES] = k_nope[:, hd * LANES:(hd + 1) * LANES].astype(BF16)
        k_ref[:, hd * HEAD_W + LANES:(hd + 1) * HEAD_W] = k_pe
    vt = lax.dot_general(wuvt_ref[...], ckv, (((1,), (1,)), ((), ())), preferred_element_type=F32)
    vt_ref[0] = vt.astype(BF16)


def _mla_proj(x, cos, sin, wdq, gq, wuq, wdkv, gkv, wkpe, wuk, wuvt, batch, seq):
    n = x.shape[0]
    tps = seq // ROW_TILE
    hw = MLA_HEADS * HEAD_W
    hv = MLA_HEADS * V_HEAD
    pos_spec = pl.BlockSpec((ROW_TILE, LANES), lambda i: (i % tps, 0))
    return pl.pallas_call(
        _mla_proj_kernel,
        grid=(n // ROW_TILE,),
        in_specs=[_row_spec(D_MODEL), pos_spec, pos_spec,
                  _const_spec((D_MODEL, Q_LORA)), _const_spec((1, Q_LORA)),
                  _const_spec((Q_LORA, 3 * MLA_HEADS * LANES)),
                  _const_spec((D_MODEL, KV_LORA)), _const_spec((1, KV_LORA)),
                  _const_spec((D_MODEL, 2 * LANES)),
                  _const_spec((KV_LORA, MLA_HEADS * QK_NOPE)), _const_spec((hv, KV_LORA))],
        out_specs=[_row_spec(hw), _row_spec(hw),
                   pl.BlockSpec((1, hv, ROW_TILE), lambda i: (i // tps, 0, i % tps))],
        out_shape=[jax.ShapeDtypeStruct((n, hw), BF16), jax.ShapeDtypeStruct((n, hw), BF16),
                   jax.ShapeDtypeStruct((batch, hv, seq), BF16)],
        compiler_params=_params("parallel"),
        name="mla_proj",
    )(x, cos, sin, wdq, gq, wuq, wdkv, gkv, wkpe, wuk, wuvt)


def _mla_attn_kernel(q_ref, k_ref, vt_ref, o_ref, *, seq):
    nt = (((1,), (1,)), ((), ()))
    key_chunk = lax.broadcasted_iota(jnp.int32, (ATT_TQ, ATT_TQ), 0) // CHUNK
    qry_chunk = lax.broadcasted_iota(jnp.int32, (ATT_TQ, ATT_TQ), 1) // CHUNK
    diag_allowed = key_chunk <= qry_chunk
    def scores(qi):
        lo, hi = qi * ATT_TQ, (qi + 1) * ATT_TQ
        q = q_ref[0, lo:hi, :]
        s_diag = lax.dot_general(k_ref[0, lo:hi, :], q, nt, preferred_element_type=F32)
        s_diag = jnp.where(diag_allowed, s_diag, -1e30)
        m = jnp.max(s_diag, axis=0, keepdims=True)
        s_off = None
        if qi > 0:
            s_off = lax.dot_general(k_ref[0, 0:lo, :], q, nt, preferred_element_type=F32)
            m = jnp.maximum(m, jnp.max(s_off, axis=0, keepdims=True))
        return s_diag, s_off, m

    def finish(qi, s_diag, s_off, m):
        lo, hi = qi * ATT_TQ, (qi + 1) * ATT_TQ
        p_diag = jnp.exp2(s_diag - m)
        l = jnp.sum(p_diag, axis=0, keepdims=True)
        acc = _dot(vt_ref[0, :, lo:hi], p_diag.astype(BF16))
        if qi > 0:
            p_off = jnp.exp2(s_off - m)
            l = l + jnp.sum(p_off, axis=0, keepdims=True)
            acc = acc + _dot(vt_ref[0, :, 0:lo], p_off.astype(BF16))
        o_ref[0, :, lo:hi] = (acc / l).astype(BF16)

    n_q = seq // ATT_TQ
    pending = scores(0)
    for qi in range(n_q):
        nxt = scores(qi + 1) if qi + 1 < n_q else None
        finish(qi, *pending)
        pending = nxt


def _mla_attn(q, k, vt, batch, seq):
    assert ATT_TQ % CHUNK == 0 and seq % ATT_TQ == 0
    qk_spec = pl.BlockSpec((1, seq, HEAD_W), lambda b, h: (b, 0, h))
    vt_spec = pl.BlockSpec((1, V_HEAD, seq), lambda b, h: (b, h, 0))
    return pl.pallas_call(
        functools.partial(_mla_attn_kernel, seq=seq),
        grid=(batch, MLA_HEADS),
        in_specs=[qk_spec, qk_spec, vt_spec],
        out_specs=vt_spec,
        out_shape=jax.ShapeDtypeStruct((batch, MLA_HEADS * V_HEAD, seq), BF16),
        compiler_params=_params("parallel", "parallel"),
        name="mla_attn",
    )(q.reshape(batch, seq, MLA_HEADS * HEAD_W), k.reshape(batch, seq, MLA_HEADS * HEAD_W), vt)


def _mla_out_kernel(x_ref, ot_ref, wo_ref, g_ref, b_ref, o_ref):
    y = lax.dot_general(ot_ref[0], wo_ref[...], (((0,), (0,)), ((), ())), preferred_element_type=F32)
    o_ref[...] = _layer_norm(ALPHA * x_ref[...] + y, g_ref[...], b_ref[...])


def _mla_out_ln(x, ot, w_o, g, b, seq):
    n = x.shape[0]
    tps = seq // ROW_TILE
    hv = MLA_HEADS * V_HEAD
    return pl.pallas_call(
        _mla_out_kernel,
        grid=(n // ROW_TILE,),
        in_specs=[_row_spec(D_MODEL), pl.BlockSpec((1, hv, ROW_TILE), lambda i: (i // tps, 0, i % tps)),
                  _const_spec((hv, D_MODEL)), _const_spec((1, D_MODEL)), _const_spec((1, D_MODEL))],
        out_specs=_row_spec(D_MODEL),
        out_shape=jax.ShapeDtypeStruct((n, D_MODEL), F32),
        compiler_params=_params("parallel"),
        name="mla_out_ln",
    )(x, ot, w_o, g, b)


def _rope_tables(seq):
    pos = jnp.arange(seq, dtype=F32)
    inv_freq = ROPE_THETA ** (-jnp.arange(0, QK_ROPE, 2, dtype=F32) / QK_ROPE)
    ang = pos[:, None] * inv_freq[None, :]
    pad = jnp.zeros((seq, LANES - QK_ROPE), F32)
    cos, sin = jnp.cos(ang), jnp.sin(ang)
    return jnp.concatenate([cos, cos, pad], axis=1), jnp.concatenate([sin, sin, pad], axis=1)


def _rotate_half_cols(w):
    half = w.shape[-1] // 2
    return jnp.concatenate([-w[..., half:], w[..., :half]], axis=-1)


def _pad_lanes(w):
    return jnp.pad(w, [(0, 0)] * (w.ndim - 1) + [(0, LANES - w.shape[-1])])


def _mla_layer(x, w_dq, g_q, w_uq, w_dkv, g_kv, w_uk, w_uv, w_o, g, b, batch, seq):
    wq = w_uq.reshape(Q_LORA, MLA_HEADS, QK_NOPE + QK_ROPE)
    wq_pe = wq[..., QK_NOPE:]
    wuq = jnp.concatenate(
        [wq[..., :QK_NOPE].reshape(Q_LORA, -1),
         _pad_lanes(wq_pe).reshape(Q_LORA, -1),
         _pad_lanes(_rotate_half_cols(wq_pe)).reshape(Q_LORA, -1)], axis=1).astype(BF16)
    wk_pe = w_dkv[:, KV_LORA:]
    wkpe = jnp.concatenate([_pad_lanes(wk_pe), _pad_lanes(_rotate_half_cols(wk_pe))], axis=1).astype(BF16)
    wuvt = w_uv.reshape(KV_LORA, MLA_HEADS * V_HEAD).T.astype(BF16)
    cos, sin = _rope_tables(seq)
    q, k, vt = _mla_proj(x, cos, sin, w_dq.astype(BF16), g_q[None], wuq, w_dkv[:, :KV_LORA].astype(BF16),
                         g_kv[None], wkpe, w_uk.reshape(KV_LORA, -1).astype(BF16), wuvt, batch, seq)
    ot = _mla_attn(q, k, vt, batch, seq)
    return _mla_out_ln(x, ot, w_o.astype(BF16), g, b, seq)


def kernel(x, sc_w_in, sc_conv_w, sc_w_out, mla_w_dq, mla_g_q, mla_w_uq, mla_w_dkv, mla_g_kv, mla_w_uk, mla_w_uv, mla_w_o, cf_w_pw1, cf_b_pw1, cf_dw_w, cf_dw_b, cf_norm_g, cf_norm_b, cf_w_pw2, cf_b_pw2, ff_w1, ff_w2, ln_mix_g, ln_mix_b, ln_ff_g, ln_ff_b):
    batch, seq, d = x.shape
    assert d == D_MODEL and seq % ROW_TILE == 0
    h = x.reshape(batch * seq, d)

    calls = []
    for i in range(DEPTH):
        m, j = i % N_MIXERS, i // N_MIXERS
        mixer_weights = {0: [(sc_w_in, j), (sc_w_out, j)], 1: [], 2: [(cf_w_pw1, j), (cf_w_pw2, j)]}[m]
        calls.append((("sconv", "mla", "conf")[m], i, mixer_weights))
        calls.append(("mlp", i, [(ff_w1, i), (ff_w2, i)]))
    jobs = [[] for _ in calls]
    source = [None] * len(calls)
    carrier = None
    for c, (kind, _, needed) in enumerate(calls):
        if needed and carrier is not None:
            source[c] = (carrier, len(jobs[carrier]))
            jobs[carrier].extend(needed)
        if kind in ("sconv", "mlp"):
            carrier = c
    casted = [None] * len(calls)

    def bf16_weights(c):
        needed = calls[c][2]
        if source[c] is None:
            return [w[layer].astype(BF16) for w, layer in needed]
        carrier, first = source[c]
        return casted[carrier][first:first + len(needed)]

    for c, (kind, i, _) in enumerate(calls):
        j = i // N_MIXERS
        g, b = ln_mix_g[i][None], ln_mix_b[i][None]
        if kind == "sconv":
            w_in, w_out = bf16_weights(c)
            h, *casted[c] = _sconv_ln(h, w_in, sc_conv_w[j], w_out, g, b, seq, cast=jobs[c])
        elif kind == "mla":
            h = _mla_layer(h, mla_w_dq[j], mla_g_q[j], mla_w_uq[j], mla_w_dkv[j], mla_g_kv[j],
                           mla_w_uk[j], mla_w_uv[j], mla_w_o[j], g, b, batch, seq)
        elif kind == "conf":
            w_pw1, w_pw2 = bf16_weights(c)
            h = _conf_ln(h, w_pw1, cf_b_pw1[j][None], cf_dw_w[j], cf_dw_b[j][None], cf_norm_g[j][None],
                         cf_norm_b[j][None], w_pw2, cf_b_pw2[j][None], g, b, seq)
        else:
            w1, w2 = bf16_weights(c)
            h, *casted[c] = _mlp_ln(h, w1, w2, ln_ff_g[i][None], ln_ff_b[i][None], cast=jobs[c])
    return h.reshape(batch, seq, d)
```

```python
import functools

import jax
import jax.numpy as jnp
from jax import lax
from jax.experimental import pallas as pl
from jax.experimental.pallas import tpu as pltpu

D_MODEL = 1024
DEPTH = 4
CHUNK = 64
N_MIXERS = 3
ALPHA = (2.0 * DEPTH) ** 0.25
LN_EPS = 1e-5
RMS_EPS = 1e-6
SC_WIDTH = 3
MLA_HEADS = 8
QK_NOPE = 128
QK_ROPE = 64
V_HEAD = 128
Q_LORA = 3 * D_MODEL // 8
KV_LORA = D_MODEL // 4
ROPE_THETA = 10000.0
CONF_WIDTH = 31
D_FF = 4 * D_MODEL

LANES = 128
SUBLANES = 8
N_SLABS = D_MODEL // LANES
HEAD_W = 2 * LANES
VMEM_LIMIT = 56 * 1024 * 1024

SUB_ROWS = 512
SUBTILES = 2
STEP_ROWS = SUBTILES * SUB_ROWS
ATT_TQ = 256
ATT_HEADS = 2
SC_HALO = 8
CF_HALO = 32
CONV_ROWS = 128
FF_CHUNKS = 4

BF16 = jnp.bfloat16
F32 = jnp.float32
LOG2_E = 1.4426950408889634
NT_DIMS = (((1,), (1,)), ((), ()))
TN_DIMS = (((0,), (0,)), ((), ()))


def _dot(a, b):
    return jnp.dot(a, b, preferred_element_type=F32)


def _layer_norm(z, g, b):
    mu = jnp.mean(z, axis=-1, keepdims=True)
    zc = z - mu
    var = jnp.mean(zc * zc, axis=-1, keepdims=True)
    return zc * lax.rsqrt(var + LN_EPS) * g + b


def _rms_norm(z, g):
    return z * lax.rsqrt(jnp.mean(z * z, axis=-1, keepdims=True) + RMS_EPS) * g


def _sub(t):
    return slice(t * SUB_ROWS, (t + 1) * SUB_ROWS)


def _const_spec(shape):
    nd = len(shape)
    return pl.BlockSpec(shape, lambda *_: (0,) * nd, pipeline_mode=pl.Buffered(1))


def _row_spec(width):
    return pl.BlockSpec((STEP_ROWS, width), lambda i: (i, 0))


def _params(*sem):
    return pltpu.CompilerParams(dimension_semantics=sem, vmem_limit_bytes=VMEM_LIMIT)


def _cast_specs(jobs, n_steps, step=lambda i: i):
    in_specs, out_specs, shapes = [], [], []
    for w, layer in jobs:
        _, rows, cols = w.shape
        assert rows % n_steps == 0 and (rows // n_steps) % (2 * SUBLANES) == 0
        in_specs.append(pl.BlockSpec((None, rows // n_steps, cols),
                                     lambda *idx, layer=layer: (layer, step(*idx), 0)))
        out_specs.append(pl.BlockSpec((rows // n_steps, cols), lambda *idx: (step(*idx), 0)))
        shapes.append(jax.ShapeDtypeStruct((rows, cols), BF16))
    return in_specs, out_specs, shapes


def _cast_blocks(src_refs, dst_refs):
    for src, dst in zip(src_refs, dst_refs):
        dst[...] = src[...].astype(BF16)


def _mlp_ln_body(x_ref, w1_ref, w2_ref, g_ref, b_ref, o_ref):
    fc = D_FF // FF_CHUNKS

    def finish(rows, y):
        o_ref[rows, :] = _layer_norm(ALPHA * x_ref[rows, :] + y, g_ref[...], b_ref[...])

    pending = None
    for t in range(SUBTILES):
        xb = x_ref[_sub(t), :].astype(BF16)
        y = None
        for c in range(FF_CHUNKS):
            h = _dot(xb, w1_ref[:, c * fc:(c + 1) * fc])
            h = jnp.maximum(h, 0.0)
            h = (h * h).astype(BF16)
            yc = _dot(h, w2_ref[c * fc:(c + 1) * fc, :])
            y = yc if y is None else y + yc
            if c == 0 and pending is not None:
                finish(*pending)
        pending = (_sub(t), y)
    finish(*pending)


def _mlp_ln_kernel(x_ref, w1_ref, w2_ref, g_ref, b_ref, *refs, n_cast):
    cast_src, o_ref, cast_dst = refs[:n_cast], refs[n_cast], refs[n_cast + 1:]
    _cast_blocks(cast_src, cast_dst)
    _mlp_ln_body(x_ref, w1_ref, w2_ref, g_ref, b_ref, o_ref)


def _attn_out_mlp_ln_kernel(x_ref, ot_ref, wo_ref, gm_ref, bm_ref, w1_ref, w2_ref, g_ref, b_ref, *refs, n_cast):
    cast_src, o_ref, cast_dst = refs[:n_cast], refs[n_cast], refs[n_cast + 1:]
    _cast_blocks(cast_src, cast_dst)
    ys = [lax.dot_general(ot_ref[0, :, _sub(t)], wo_ref[...], TN_DIMS, preferred_element_type=F32)
          for t in range(SUBTILES)]
    for t in range(SUBTILES):
        o_ref[_sub(t), :] = _layer_norm(ALPHA * x_ref[_sub(t), :] + ys[t], gm_ref[...], bm_ref[...])
    _mlp_ln_body(o_ref, w1_ref, w2_ref, g_ref, b_ref, o_ref)


def _mlp_ln(x, w1, w2, g, b, cast=(), attn_out=None, seq=None):
    n = x.shape[0]
    n_steps = n // STEP_ROWS
    cast_in, cast_out, cast_shapes = _cast_specs(cast, n_steps)
    vec = _const_spec((1, D_MODEL))
    mlp_specs = [_const_spec((D_MODEL, D_FF)), _const_spec((D_FF, D_MODEL)), vec, vec]
    if attn_out is None:
        body, pre_specs, pre_args = _mlp_ln_kernel, [], []
    else:
        sps = seq // STEP_ROWS
        hv = MLA_HEADS * V_HEAD
        body = _attn_out_mlp_ln_kernel
        pre_specs = [pl.BlockSpec((1, hv, STEP_ROWS), lambda i: (i // sps, 0, i % sps)),
                     _const_spec((hv, D_MODEL)), vec, vec]
        pre_args = list(attn_out)
    return pl.pallas_call(
        functools.partial(body, n_cast=len(cast)),
        grid=(n_steps,),
        in_specs=[_row_spec(D_MODEL)] + pre_specs + mlp_specs + cast_in,
        out_specs=[_row_spec(D_MODEL)] + cast_out,
        out_shape=[jax.ShapeDtypeStruct((n, D_MODEL), F32)] + cast_shapes,
        compiler_params=_params("parallel"),
        name="mlp_ln" if attn_out is None else "attn_out_mlp_ln",
    )(x, *pre_args, w1, w2, g, b, *[w for w, _ in cast])


def _sconv_kernel(x_ref, win_ref, cw_ref, wout_ref, g_ref, b_ref, *refs, steps_per_seq, n_cast):
    cast_src, o_ref, cast_dst = refs[:n_cast], refs[n_cast], refs[n_cast + 1:2 * n_cast + 1]
    ubuf, gbuf = refs[2 * n_cast + 1:]
    _cast_blocks(cast_src, cast_dst)

    @pl.when(pl.program_id(0) % steps_per_seq == 0)
    def _():
        ubuf[:, 0:SC_HALO, :] = jnp.zeros((N_SLABS, SC_HALO, LANES), F32)

    b_gates = []
    for t in range(SUBTILES):
        xb = x_ref[_sub(t), :].astype(BF16)
        b_gates.append(_dot(xb, win_ref[:, 0:D_MODEL]))
        u = _dot(xb, win_ref[:, D_MODEL:2 * D_MODEL]) * _dot(xb, win_ref[:, 2 * D_MODEL:3 * D_MODEL])
        for s in range(N_SLABS):
            ubuf[s, SC_HALO + t * SUB_ROWS:SC_HALO + (t + 1) * SUB_ROWS, :] = u[:, s * LANES:(s + 1) * LANES]

    ys = []
    for t in range(SUBTILES):
        for s in range(N_SLABS):
            cols = slice(s * LANES, (s + 1) * LANES)
            conv = None
            for k in range(SC_WIDTH):
                start = SC_HALO + t * SUB_ROWS - (SC_WIDTH - 1) + k
                term = cw_ref[k:k + 1, cols] * ubuf[s, start:start + SUB_ROWS, :]
                conv = term if conv is None else conv + term
            gbuf[_sub(t), cols] = (b_gates[t][:, cols] * conv).astype(BF16)
        ys.append(_dot(gbuf[_sub(t), :], wout_ref[...]))

    for t in range(SUBTILES):
        o_ref[_sub(t), :] = _layer_norm(ALPHA * x_ref[_sub(t), :] + ys[t], g_ref[...], b_ref[...])

    ubuf[:, 0:SC_HALO, :] = ubuf[:, STEP_ROWS:STEP_ROWS + SC_HALO, :]


def _sconv_ln(x, w_in, conv_w, w_out, g, b, seq, cast=()):
    n = x.shape[0]
    n_steps = n // STEP_ROWS
    cast_in, cast_out, cast_shapes = _cast_specs(cast, n_steps)
    return pl.pallas_call(
        functools.partial(_sconv_kernel, steps_per_seq=seq // STEP_ROWS, n_cast=len(cast)),
        grid=(n_steps,),
        in_specs=[_row_spec(D_MODEL), _const_spec((D_MODEL, 3 * D_MODEL)), _const_spec((SC_WIDTH, D_MODEL)),
                  _const_spec((D_MODEL, D_MODEL)), _const_spec((1, D_MODEL)), _const_spec((1, D_MODEL))]
                 + cast_in,
        out_specs=[_row_spec(D_MODEL)] + cast_out,
        out_shape=[jax.ShapeDtypeStruct((n, D_MODEL), F32)] + cast_shapes,
        scratch_shapes=[pltpu.VMEM((N_SLABS, SC_HALO + STEP_ROWS, LANES), F32),
                        pltpu.VMEM((STEP_ROWS, D_MODEL), BF16)],
        compiler_params=_params("arbitrary"),
        name="sconv_ln",
    )(x, w_in, conv_w, w_out, g, b, *[w for w, _ in cast])


def _conf_kernel(x_ref, w1_ref, b1_ref, dww_ref, dwb_ref, ng_ref, nb_ref, w2_ref, b2_ref, g_ref, b_ref,
                 o_ref, hbuf, cbuf, *, steps_per_seq):
    @pl.when(pl.program_id(0) % steps_per_seq == 0)
    def _():
        hbuf[:, 0:CF_HALO, :] = jnp.zeros((N_SLABS, CF_HALO, LANES), F32)

    for t in range(SUBTILES):
        xb = x_ref[_sub(t), :].astype(BF16)
        a = _dot(xb, w1_ref[:, 0:D_MODEL]) + b1_ref[:, 0:D_MODEL]
        gate = _dot(xb, w1_ref[:, D_MODEL:2 * D_MODEL]) + b1_ref[:, D_MODEL:2 * D_MODEL]
        h = a * jax.nn.sigmoid(gate)
        for s in range(N_SLABS):
            hbuf[s, CF_HALO + t * SUB_ROWS:CF_HALO + (t + 1) * SUB_ROWS, :] = h[:, s * LANES:(s + 1) * LANES]

    for t in range(SUBTILES):
        for s in range(N_SLABS):
            cols = slice(s * LANES, (s + 1) * LANES)
            for r0 in range(t * SUB_ROWS, (t + 1) * SUB_ROWS, CONV_ROWS):
                acc = jnp.zeros((CONV_ROWS, LANES), F32) + dwb_ref[:, cols]
                for k in range(CONF_WIDTH):
                    start = CF_HALO + r0 - (CONF_WIDTH - 1) + k
                    acc = acc + dww_ref[k:k + 1, cols] * hbuf[s, start:start + CONV_ROWS, :]
                cbuf[r0:r0 + CONV_ROWS, cols] = acc
        hn = _layer_norm(cbuf[_sub(t), :], ng_ref[...], nb_ref[...])
        hs = (hn * jax.nn.sigmoid(hn)).astype(BF16)
        y = _dot(hs, w2_ref[...]) + b2_ref[...]
        o_ref[_sub(t), :] = _layer_norm(ALPHA * x_ref[_sub(t), :] + y, g_ref[...], b_ref[...])

    hbuf[:, 0:CF_HALO, :] = hbuf[:, STEP_ROWS:STEP_ROWS + CF_HALO, :]


def _conf_ln(x, w_pw1, b_pw1, dw_w, dw_b, norm_g, norm_b, w_pw2, b_pw2, g, b, seq):
    n = x.shape[0]
    vec = _const_spec((1, D_MODEL))
    return pl.pallas_call(
        functools.partial(_conf_kernel, steps_per_seq=seq // STEP_ROWS),
        grid=(n // STEP_ROWS,),
        in_specs=[_row_spec(D_MODEL), _const_spec((D_MODEL, 2 * D_MODEL)), _const_spec((1, 2 * D_MODEL)),
                  _const_spec((CONF_WIDTH, D_MODEL)), vec, vec, vec,
                  _const_spec((D_MODEL, D_MODEL)), vec, vec, vec],
        out_specs=_row_spec(D_MODEL),
        out_shape=jax.ShapeDtypeStruct((n, D_MODEL), F32),
        scratch_shapes=[pltpu.VMEM((N_SLABS, CF_HALO + STEP_ROWS, LANES), F32),
                        pltpu.VMEM((STEP_ROWS, D_MODEL), F32)],
        compiler_params=_params("arbitrary"),
        name="conf_ln",
    )(x, w_pw1, b_pw1, dw_w, dw_b, norm_g, norm_b, w_pw2, b_pw2, g, b)


def _mla_proj_kernel(x_ref, cos_ref, sin_ref, wdq_ref, gq_ref, wuq_ref, wdkv_ref, gkv_ref, wkpe_ref,
                     wuk_ref, wuvt_ref, q_ref, k_ref, vt_ref):
    nope_w = MLA_HEADS * QK_NOPE
    rope_w = MLA_HEADS * QK_ROPE
    scale = (QK_NOPE + QK_ROPE) ** -0.5 * LOG2_E
    first_half = lax.broadcasted_iota(jnp.int32, (SUB_ROWS, LANES), 1) < QK_ROPE
    for t in range(SUBTILES):
        xb = x_ref[_sub(t), :].astype(BF16)
        cos = cos_ref[_sub(t), :]
        sin = sin_ref[_sub(t), :]

        cq = _rms_norm(_dot(xb, wdq_ref[...]), gq_ref[...]).astype(BF16)
        qall = _dot(cq, wuq_ref[...])
        for hd in range(MLA_HEADS):
            q_ref[_sub(t), hd * HEAD_W:hd * HEAD_W + LANES] = (
                qall[:, hd * QK_NOPE:(hd + 1) * QK_NOPE] * scale).astype(BF16)
        for pair in range(MLA_HEADS // 2):
            pe = qall[:, nope_w + pair * LANES:nope_w + (pair + 1) * LANES]
            pe_rot = qall[:, nope_w + rope_w + pair * LANES:nope_w + rope_w + (pair + 1) * LANES]
            slab = ((pe * cos + pe_rot * sin) * scale).astype(BF16)
            for hd in (2 * pair, 2 * pair + 1):
                q_ref[_sub(t), hd * HEAD_W + LANES:(hd + 1) * HEAD_W] = slab

        ckv = _rms_norm(_dot(xb, wdkv_ref[...]), gkv_ref[...]).astype(BF16)
        kpe2 = _dot(xb, wkpe_ref[...])
        k_pe = kpe2[:, 0:LANES] * cos + kpe2[:, LANES:2 * LANES] * sin
        k_pe_even = jnp.where(first_half, k_pe, 0.0).astype(BF16)
        k_pe_odd = jnp.where(first_half, 0.0, k_pe).astype(BF16)
        k_nope = _dot(ckv, wuk_ref[...])
        for hd in range(MLA_HEADS):
            k_ref[_sub(t), hd * HEAD_W:hd * HEAD_W + LANES] = (
                k_nope[:, hd * QK_NOPE:(hd + 1) * QK_NOPE].astype(BF16))
            k_ref[_sub(t), hd * HEAD_W + LANES:(hd + 1) * HEAD_W] = k_pe_even if hd % 2 == 0 else k_pe_odd
        vt = lax.dot_general(wuvt_ref[...], ckv, NT_DIMS, preferred_element_type=F32)
        vt_ref[0, :, _sub(t)] = vt.astype(BF16)


def _mla_proj(x, cos, sin, wdq, gq, wuq, wdkv, gkv, wkpe, wuk, wuvt, batch, seq):
    n = x.shape[0]
    sps = seq // STEP_ROWS
    hw = MLA_HEADS * HEAD_W
    hv = MLA_HEADS * V_HEAD
    pos_spec = pl.BlockSpec((STEP_ROWS, LANES), lambda i: (i % sps, 0))
    return pl.pallas_call(
        _mla_proj_kernel,
        grid=(n // STEP_ROWS,),
        in_specs=[_row_spec(D_MODEL), pos_spec, pos_spec,
                  _const_spec((D_MODEL, Q_LORA)), _const_spec((1, Q_LORA)),
                  _const_spec((Q_LORA, MLA_HEADS * (QK_NOPE + 2 * QK_ROPE))),
                  _const_spec((D_MODEL, KV_LORA)), _const_spec((1, KV_LORA)),
                  _const_spec((D_MODEL, 2 * LANES)),
                  _const_spec((KV_LORA, MLA_HEADS * QK_NOPE)), _const_spec((hv, KV_LORA))],
        out_specs=[_row_spec(hw), _row_spec(hw),
                   pl.BlockSpec((1, hv, STEP_ROWS), lambda i: (i // sps, 0, i % sps))],
        out_shape=[jax.ShapeDtypeStruct((n, hw), BF16), jax.ShapeDtypeStruct((n, hw), BF16),
                   jax.ShapeDtypeStruct((batch, hv, seq), BF16)],
        compiler_params=_params("parallel"),
        name="mla_proj",
    )(x, cos, sin, wdq, gq, wuq, wdkv, gkv, wkpe, wuk, wuvt)


def _mla_attn_kernel(q_ref, k_ref, vt_ref, *refs, seq, n_cast):
    cast_src, o_ref, cast_dst = refs[:n_cast], refs[n_cast], refs[n_cast + 1:]
    _cast_blocks(cast_src, cast_dst)
    key_chunk = lax.broadcasted_iota(jnp.int32, (ATT_TQ, ATT_TQ), 0) // CHUNK
    qry_chunk = lax.broadcasted_iota(jnp.int32, (ATT_TQ, ATT_TQ), 1) // CHUNK
    diag_allowed = key_chunk <= qry_chunk

    def scores(hd, qi):
        lo, hi = qi * ATT_TQ, (qi + 1) * ATT_TQ
        hcols = slice(hd * HEAD_W, (hd + 1) * HEAD_W)
        q = q_ref[0, lo:hi, hcols]
        s_diag = lax.dot_general(k_ref[0, lo:hi, hcols], q, NT_DIMS, preferred_element_type=F32)
        s_diag = jnp.where(diag_allowed, s_diag, -1e30)
        m = jnp.max(s_diag, axis=0, keepdims=True)
        s_off = None
        if qi > 0:
            s_off = lax.dot_general(k_ref[0, 0:lo, hcols], q, NT_DIMS, preferred_element_type=F32)
            m = jnp.maximum(m, jnp.max(s_off, axis=0, keepdims=True))
        return s_diag, s_off, m

    def finish(hd, qi, s_diag, s_off, m):
        lo, hi = qi * ATT_TQ, (qi + 1) * ATT_TQ
        vrows = slice(hd * V_HEAD, (hd + 1) * V_HEAD)
        p_diag = jnp.exp2(s_diag - m)
        l = jnp.sum(p_diag, axis=0, keepdims=True)
        acc = _dot(vt_ref[0, vrows, lo:hi], p_diag.astype(BF16))
        if qi > 0:
            p_off = jnp.exp2(s_off - m)
            l = l + jnp.sum(p_off, axis=0, keepdims=True)
            acc = acc + _dot(vt_ref[0, vrows, 0:lo], p_off.astype(BF16))
        o_ref[0, vrows, lo:hi] = (acc / l).astype(BF16)

    items = [(hd, qi) for qi in range(seq // ATT_TQ) for hd in range(ATT_HEADS)]
    pending = scores(*items[0])
    for n, item in enumerate(items):
        nxt = scores(*items[n + 1]) if n + 1 < len(items) else None
        finish(*item, *pending)
        pending = nxt


def _mla_attn(q, k, vt, batch, seq, cast=()):
    assert ATT_TQ % CHUNK == 0 and seq % ATT_TQ == 0 and MLA_HEADS % ATT_HEADS == 0
    head_groups = MLA_HEADS // ATT_HEADS
    qk_spec = pl.BlockSpec((1, seq, ATT_HEADS * HEAD_W), lambda b, h: (b, 0, h))
    vt_spec = pl.BlockSpec((1, ATT_HEADS * V_HEAD, seq), lambda b, h: (b, h, 0))
    cast_in, cast_out, cast_shapes = _cast_specs(cast, batch * head_groups, lambda b, h: b * head_groups + h)
    return pl.pallas_call(
        functools.partial(_mla_attn_kernel, seq=seq, n_cast=len(cast)),
        grid=(batch, head_groups),
        in_specs=[qk_spec, qk_spec, vt_spec] + cast_in,
        out_specs=[vt_spec] + cast_out,
        out_shape=[jax.ShapeDtypeStruct((batch, MLA_HEADS * V_HEAD, seq), BF16)] + cast_shapes,
        compiler_params=_params("parallel", "parallel"),
        name="mla_attn",
    )(q.reshape(batch, seq, MLA_HEADS * HEAD_W), k.reshape(batch, seq, MLA_HEADS * HEAD_W), vt,
      *[w for w, _ in cast])


def _rope_tables(seq):
    pos = jnp.arange(seq, dtype=F32)
    inv_freq = ROPE_THETA ** (-jnp.arange(0, QK_ROPE, 2, dtype=F32) / QK_ROPE)
    ang = pos[:, None] * inv_freq[None, :]
    reps = LANES // (QK_ROPE // 2)
    return jnp.tile(jnp.cos(ang), (1, reps)), jnp.tile(jnp.sin(ang), (1, reps))


def _rotate_half_cols(w):
    half = w.shape[-1] // 2
    return jnp.concatenate([-w[..., half:], w[..., :half]], axis=-1)


def _mla_attention(x, w_dq, g_q, w_uq, w_dkv, g_kv, w_uk, w_uv, batch, seq, cast=()):
    wq = w_uq.reshape(Q_LORA, MLA_HEADS, QK_NOPE + QK_ROPE)
    wq_pe = wq[..., QK_NOPE:]
    wuq = jnp.concatenate(
        [wq[..., :QK_NOPE].reshape(Q_LORA, -1), wq_pe.reshape(Q_LORA, -1),
         _rotate_half_cols(wq_pe).reshape(Q_LORA, -1)], axis=1).astype(BF16)
    wk_pe = w_dkv[:, KV_LORA:]
    wk_pe_rot = _rotate_half_cols(wk_pe)
    wkpe = jnp.concatenate([wk_pe, wk_pe, wk_pe_rot, wk_pe_rot], axis=1).astype(BF16)
    wuvt = w_uv.reshape(KV_LORA, MLA_HEADS * V_HEAD).T.astype(BF16)
    cos, sin = _rope_tables(seq)
    q, k, vt = _mla_proj(x, cos, sin, w_dq.astype(BF16), g_q[None], wuq, w_dkv[:, :KV_LORA].astype(BF16),
                         g_kv[None], wkpe, w_uk.reshape(KV_LORA, -1).astype(BF16), wuvt, batch, seq)
    return _mla_attn(q, k, vt, batch, seq, cast=cast)


def kernel(x, sc_w_in, sc_conv_w, sc_w_out, mla_w_dq, mla_g_q, mla_w_uq, mla_w_dkv, mla_g_kv, mla_w_uk, mla_w_uv, mla_w_o, cf_w_pw1, cf_b_pw1, cf_dw_w, cf_dw_b, cf_norm_g, cf_norm_b, cf_w_pw2, cf_b_pw2, ff_w1, ff_w2, ln_mix_g, ln_mix_b, ln_ff_g, ln_ff_b):
    batch, seq, d = x.shape
    assert d == D_MODEL and seq % STEP_ROWS == 0
    h = x.reshape(batch * seq, d)

    calls = []
    for i in range(DEPTH):
        m, j = i % N_MIXERS, i // N_MIXERS
        mixer_weights = {0: [(sc_w_in, j), (sc_w_out, j)], 1: [], 2: [(cf_w_pw1, j), (cf_w_pw2, j)]}[m]
        calls.append((("sconv", "mla", "conf")[m], i, mixer_weights))
        calls.append(("mlp", i, [(ff_w1, i), (ff_w2, i)]))
    jobs = [[] for _ in calls]
    source = [None] * len(calls)
    carrier = None
    for c, (kind, _, needed) in enumerate(calls):
        if needed and carrier is not None:
            source[c] = (carrier, len(jobs[carrier]))
            jobs[carrier].extend(needed)
        if kind in ("sconv", "mla") or (kind == "mlp" and calls[c - 1][0] != "mla"):
            carrier = c
    casted = [None] * len(calls)

    def bf16_weights(c):
        needed = calls[c][2]
        if source[c] is None:
            return [w[layer].astype(BF16) for w, layer in needed]
        carrier, first = source[c]
        return casted[carrier][first:first + len(needed)]

    attn_out = None
    for c, (kind, i, _) in enumerate(calls):
        j = i // N_MIXERS
        g, b = ln_mix_g[i][None], ln_mix_b[i][None]
        if kind == "sconv":
            w_in, w_out = bf16_weights(c)
            h, *casted[c] = _sconv_ln(h, w_in, sc_conv_w[j], w_out, g, b, seq, cast=jobs[c])
        elif kind == "mla":
            ot, *casted[c] = _mla_attention(h, mla_w_dq[j], mla_g_q[j], mla_w_uq[j], mla_w_dkv[j], mla_g_kv[j],
                                            mla_w_uk[j], mla_w_uv[j], batch, seq, cast=jobs[c])
            attn_out = (ot, mla_w_o[j].astype(BF16), g, b)
        elif kind == "conf":
            w_pw1, w_pw2 = bf16_weights(c)
            h = _conf_ln(h, w_pw1, cf_b_pw1[j][None], cf_dw_w[j], cf_dw_b[j][None], cf_norm_g[j][None],
                         cf_norm_b[j][None], w_pw2, cf_b_pw2[j][None], g, b, seq)
        else:
            w1, w2 = bf16_weights(c)
            h, *casted[c] = _mlp_ln(h, w1, w2, ln_ff_g[i][None], ln_ff_b[i][None], cast=jobs[c],
                                    attn_out=attn_out, seq=seq)
            attn_out = None
    return h.reshape(batch, seq, d)
```

```python
import functools

import jax
import jax.numpy as jnp
from jax import lax
from jax.experimental import pallas as pl
from jax.experimental.pallas import tpu as pltpu

D_MODEL = 1024
DEPTH = 4
CHUNK = 64
N_MIXERS = 3
ALPHA = (2.0 * DEPTH) ** 0.25
LN_EPS = 1e-5
RMS_EPS = 1e-6
SC_WIDTH = 3
MLA_HEADS = 8
QK_NOPE = 128
QK_ROPE = 64
V_HEAD = 128
Q_LORA = 3 * D_MODEL // 8
KV_LORA = D_MODEL // 4
ROPE_THETA = 10000.0
CONF_WIDTH = 31
D_FF = 4 * D_MODEL

LANES = 128
SUBLANES = 8
N_SLABS = D_MODEL // LANES
HEAD_W = 2 * LANES
VMEM_LIMIT = 60 * 1024 * 1024

SUB_ROWS = 512
SUBTILES = 2
STEP_ROWS = SUBTILES * SUB_ROWS
ATT_TQ = 256
ATT_HEADS = 2
SC_HALO = 8
CF_HALO = 32
CONV_ROWS = 128
FF_CHUNKS = 4

BF16 = jnp.bfloat16
F32 = jnp.float32
LOG2_E = 1.4426950408889634
NT_DIMS = (((1,), (1,)), ((), ()))
TN_DIMS = (((0,), (0,)), ((), ()))


def _dot(a, b):
    return jnp.dot(a, b, preferred_element_type=F32)


def _layer_norm(z, g, b):
    mu = jnp.mean(z, axis=-1, keepdims=True)
    zc = z - mu
    var = jnp.mean(zc * zc, axis=-1, keepdims=True)
    return zc * lax.rsqrt(var + LN_EPS) * g + b


def _rms_norm(z, g):
    return z * lax.rsqrt(jnp.mean(z * z, axis=-1, keepdims=True) + RMS_EPS) * g


def _sub(t):
    return slice(t * SUB_ROWS, (t + 1) * SUB_ROWS)


def _const_spec(shape):
    nd = len(shape)
    return pl.BlockSpec(shape, lambda *_: (0,) * nd, pipeline_mode=pl.Buffered(1))


def _row_spec(width):
    return pl.BlockSpec((STEP_ROWS, width), lambda i: (i, 0))


def _params(*sem):
    return pltpu.CompilerParams(dimension_semantics=sem, vmem_limit_bytes=VMEM_LIMIT)


def _cast_specs(jobs, n_steps, step=lambda i: i):
    in_specs, out_specs, shapes = [], [], []
    for w, layer in jobs:
        _, rows, cols = w.shape
        assert rows % n_steps == 0 and (rows // n_steps) % (2 * SUBLANES) == 0
        in_specs.append(pl.BlockSpec((None, rows // n_steps, cols),
                                     lambda *idx, layer=layer: (layer, step(*idx), 0)))
        out_specs.append(pl.BlockSpec((rows // n_steps, cols), lambda *idx: (step(*idx), 0)))
        shapes.append(jax.ShapeDtypeStruct((rows, cols), BF16))
    return in_specs, out_specs, shapes


def _cast_blocks(src_refs, dst_refs):
    for src, dst in zip(src_refs, dst_refs):
        dst[...] = src[...].astype(BF16)


def _mlp_stage(x_ref, w1_ref, w2_ref, g_ref, b_ref, o_ref):
    fc = D_FF // FF_CHUNKS

    def finish(rows, y):
        o_ref[rows, :] = _layer_norm(ALPHA * x_ref[rows, :] + y, g_ref[...], b_ref[...])

    pending = None
    for t in range(SUBTILES):
        xb = x_ref[_sub(t), :].astype(BF16)
        y = None
        for c in range(FF_CHUNKS):
            h = _dot(xb, w1_ref[:, c * fc:(c + 1) * fc])
            h = jnp.maximum(h, 0.0)
            h = (h * h).astype(BF16)
            yc = _dot(h, w2_ref[c * fc:(c + 1) * fc, :])
            y = yc if y is None else y + yc
            if c == 0 and pending is not None:
                finish(*pending)
        pending = (_sub(t), y)
    finish(*pending)


def _sconv_stage(x_ref, win_ref, cw_ref, wout_ref, g_ref, b_ref, o_ref, ubuf, gbuf, *, steps_per_seq):
    @pl.when(pl.program_id(0) % steps_per_seq == 0)
    def _():
        ubuf[:, 0:SC_HALO, :] = jnp.zeros((N_SLABS, SC_HALO, LANES), F32)

    b_gates = []
    for t in range(SUBTILES):
        xb = x_ref[_sub(t), :].astype(BF16)
        b_gates.append(_dot(xb, win_ref[:, 0:D_MODEL]))
        u = _dot(xb, win_ref[:, D_MODEL:2 * D_MODEL]) * _dot(xb, win_ref[:, 2 * D_MODEL:3 * D_MODEL])
        for s in range(N_SLABS):
            ubuf[s, SC_HALO + t * SUB_ROWS:SC_HALO + (t + 1) * SUB_ROWS, :] = u[:, s * LANES:(s + 1) * LANES]

    ys = []
    for t in range(SUBTILES):
        for s in range(N_SLABS):
            cols = slice(s * LANES, (s + 1) * LANES)
            conv = None
            for k in range(SC_WIDTH):
                start = SC_HALO + t * SUB_ROWS - (SC_WIDTH - 1) + k
                term = cw_ref[k:k + 1, cols] * ubuf[s, start:start + SUB_ROWS, :]
                conv = term if conv is None else conv + term
            gbuf[_sub(t), cols] = (b_gates[t][:, cols] * conv).astype(BF16)
        ys.append(_dot(gbuf[_sub(t), :], wout_ref[...]))

    for t in range(SUBTILES):
        o_ref[_sub(t), :] = _layer_norm(ALPHA * x_ref[_sub(t), :] + ys[t], g_ref[...], b_ref[...])

    ubuf[:, 0:SC_HALO, :] = ubuf[:, STEP_ROWS:STEP_ROWS + SC_HALO, :]


def _conf_stage(x_ref, w1_ref, b1_ref, dww_ref, dwb_ref, ng_ref, nb_ref, w2_ref, b2_ref, g_ref, b_ref,
                o_ref, hbuf, cbuf, *, steps_per_seq):
    @pl.when(pl.program_id(0) % steps_per_seq == 0)
    def _():
        hbuf[:, 0:CF_HALO, :] = jnp.zeros((N_SLABS, CF_HALO, LANES), F32)

    for t in range(SUBTILES):
        xb = x_ref[_sub(t), :].astype(BF16)
        a = _dot(xb, w1_ref[:, 0:D_MODEL]) + b1_ref[:, 0:D_MODEL]
        gate = _dot(xb, w1_ref[:, D_MODEL:2 * D_MODEL]) + b1_ref[:, D_MODEL:2 * D_MODEL]
        h = a * jax.nn.sigmoid(gate)
        for s in range(N_SLABS):
            hbuf[s, CF_HALO + t * SUB_ROWS:CF_HALO + (t + 1) * SUB_ROWS, :] = h[:, s * LANES:(s + 1) * LANES]

    for t in range(SUBTILES):
        for s in range(N_SLABS):
            cols = slice(s * LANES, (s + 1) * LANES)
            for r0 in range(t * SUB_ROWS, (t + 1) * SUB_ROWS, CONV_ROWS):
                acc = jnp.zeros((CONV_ROWS, LANES), F32) + dwb_ref[:, cols]
                for k in range(CONF_WIDTH):
                    start = CF_HALO + r0 - (CONF_WIDTH - 1) + k
                    acc = acc + dww_ref[k:k + 1, cols] * hbuf[s, start:start + CONV_ROWS, :]
                cbuf[r0:r0 + CONV_ROWS, cols] = acc
        hn = _layer_norm(cbuf[_sub(t), :], ng_ref[...], nb_ref[...])
        hs = (hn * jax.nn.sigmoid(hn)).astype(BF16)
        y = _dot(hs, w2_ref[...]) + b2_ref[...]
        o_ref[_sub(t), :] = _layer_norm(ALPHA * x_ref[_sub(t), :] + y, g_ref[...], b_ref[...])

    hbuf[:, 0:CF_HALO, :] = hbuf[:, STEP_ROWS:STEP_ROWS + CF_HALO, :]


def _attn_out_stage(x_ref, ot_ref, wo_ref, g_ref, b_ref, o_ref):
    ys = [lax.dot_general(ot_ref[0, :, _sub(t)], wo_ref[...], TN_DIMS, preferred_element_type=F32)
          for t in range(SUBTILES)]
    for t in range(SUBTILES):
        o_ref[_sub(t), :] = _layer_norm(ALPHA * x_ref[_sub(t), :] + ys[t], g_ref[...], b_ref[...])


def _sublayer_kernel(x_ref, *refs, mixer, n_mixer_in, with_mlp, n_cast):
    mixer_in, refs = refs[:n_mixer_in], refs[n_mixer_in:]
    n_mlp_in = 4 if with_mlp else 0
    mlp_in, refs = refs[:n_mlp_in], refs[n_mlp_in:]
    cast_src, o_ref, cast_dst, scratch = refs[:n_cast], refs[n_cast], refs[n_cast + 1:2 * n_cast + 1], refs[2 * n_cast + 1:]
    _cast_blocks(cast_src, cast_dst)
    if mixer is not None:
        mixer(x_ref, *mixer_in, o_ref, *scratch)
    if with_mlp:
        _mlp_stage(x_ref if mixer is None else o_ref, *mlp_in, o_ref)


def _sublayer_call(name, x, mixer=None, mixer_args=(), mixer_specs=(), scratch=(), mlp=None, cast=()):
    n = x.shape[0]
    n_steps = n // STEP_ROWS
    vec = _const_spec((1, D_MODEL))
    mlp_specs = [] if mlp is None else [_const_spec((D_MODEL, D_FF)), _const_spec((D_FF, D_MODEL)), vec, vec]
    cast_in, cast_out, cast_shapes = _cast_specs(cast, n_steps)
    return pl.pallas_call(
        functools.partial(_sublayer_kernel, mixer=mixer, n_mixer_in=len(mixer_args),
                          with_mlp=mlp is not None, n_cast=len(cast)),
        grid=(n_steps,),
        in_specs=[_row_spec(D_MODEL)] + list(mixer_specs) + mlp_specs + cast_in,
        out_specs=[_row_spec(D_MODEL)] + cast_out,
        out_shape=[jax.ShapeDtypeStruct((n, D_MODEL), F32)] + cast_shapes,
        scratch_shapes=list(scratch),
        compiler_params=_params("parallel" if mixer is None else "arbitrary"),
        name=name,
    )(x, *mixer_args, *(mlp or ()), *[w for w, _ in cast])


def _sconv_mixer(w_in, conv_w, w_out, g, b, seq):
    vec = _const_spec((1, D_MODEL))
    return dict(
        mixer=functools.partial(_sconv_stage, steps_per_seq=seq // STEP_ROWS),
        mixer_args=(w_in, conv_w, w_out, g, b),
        mixer_specs=(_const_spec((D_MODEL, 3 * D_MODEL)), _const_spec((SC_WIDTH, D_MODEL)),
                     _const_spec((D_MODEL, D_MODEL)), vec, vec),
        scratch=(pltpu.VMEM((N_SLABS, SC_HALO + STEP_ROWS, LANES), F32), pltpu.VMEM((STEP_ROWS, D_MODEL), BF16)))


def _conf_mixer(w_pw1, b_pw1, dw_w, dw_b, norm_g, norm_b, w_pw2, b_pw2, g, b, seq):
    vec = _const_spec((1, D_MODEL))
    return dict(
        mixer=functools.partial(_conf_stage, steps_per_seq=seq // STEP_ROWS),
        mixer_args=(w_pw1, b_pw1, dw_w, dw_b, norm_g, norm_b, w_pw2, b_pw2, g, b),
        mixer_specs=(_const_spec((D_MODEL, 2 * D_MODEL)), _const_spec((1, 2 * D_MODEL)),
                     _const_spec((CONF_WIDTH, D_MODEL)), vec, vec, vec, _const_spec((D_MODEL, D_MODEL)), vec, vec, vec),
        scratch=(pltpu.VMEM((N_SLABS, CF_HALO + STEP_ROWS, LANES), F32), pltpu.VMEM((STEP_ROWS, D_MODEL), F32)))


def _attn_out_mixer(ot, w_o, g, b, seq):
    sps = seq // STEP_ROWS
    hv = MLA_HEADS * V_HEAD
    vec = _const_spec((1, D_MODEL))
    return dict(
        mixer=_attn_out_stage,
        mixer_args=(ot, w_o, g, b),
        mixer_specs=(pl.BlockSpec((1, hv, STEP_ROWS), lambda i: (i // sps, 0, i % sps)),
                     _const_spec((hv, D_MODEL)), vec, vec))


def _mla_proj_kernel(x_ref, cos_ref, sin_ref, wdq_ref, gq_ref, wuq_ref, wdkv_ref, gkv_ref, wkpe_ref,
                     wuk_ref, wuvt_ref, q_ref, k_ref, vt_ref):
    nope_w = MLA_HEADS * QK_NOPE
    rope_w = MLA_HEADS * QK_ROPE
    scale = (QK_NOPE + QK_ROPE) ** -0.5 * LOG2_E
    first_half = lax.broadcasted_iota(jnp.int32, (SUB_ROWS, LANES), 1) < QK_ROPE
    for t in range(SUBTILES):
        xb = x_ref[_sub(t), :].astype(BF16)
        cos = cos_ref[_sub(t), :]
        sin = sin_ref[_sub(t), :]

        cq = _rms_norm(_dot(xb, wdq_ref[...]), gq_ref[...]).astype(BF16)
        qall = _dot(cq, wuq_ref[...])
        for hd in range(MLA_HEADS):
            q_ref[_sub(t), hd * HEAD_W:hd * HEAD_W + LANES] = (
                qall[:, hd * QK_NOPE:(hd + 1) * QK_NOPE] * scale).astype(BF16)
        for pair in range(MLA_HEADS // 2):
            pe = qall[:, nope_w + pair * LANES:nope_w + (pair + 1) * LANES]
            pe_rot = qall[:, nope_w + rope_w + pair * LANES:nope_w + rope_w + (pair + 1) * LANES]
            slab = ((pe * cos + pe_rot * sin) * scale).astype(BF16)
            for hd in (2 * pair, 2 * pair + 1):
                q_ref[_sub(t), hd * HEAD_W + LANES:(hd + 1) * HEAD_W] = slab

        ckv = _rms_norm(_dot(xb, wdkv_ref[...]), gkv_ref[...]).astype(BF16)
        kpe2 = _dot(xb, wkpe_ref[...])
        k_pe = kpe2[:, 0:LANES] * cos + kpe2[:, LANES:2 * LANES] * sin
        k_pe_even = jnp.where(first_half, k_pe, 0.0).astype(BF16)
        k_pe_odd = jnp.where(first_half, 0.0, k_pe).astype(BF16)
        k_nope = _dot(ckv, wuk_ref[...])
        for hd in range(MLA_HEADS):
            k_ref[_sub(t), hd * HEAD_W:hd * HEAD_W + LANES] = (
                k_nope[:, hd * QK_NOPE:(hd + 1) * QK_NOPE].astype(BF16))
            k_ref[_sub(t), hd * HEAD_W + LANES:(hd + 1) * HEAD_W] = k_pe_even if hd % 2 == 0 else k_pe_odd
        vt = lax.dot_general(wuvt_ref[...], ckv, NT_DIMS, preferred_element_type=F32)
        vt_ref[0, :, _sub(t)] = vt.astype(BF16)


def _mla_proj(x, cos, sin, wdq, gq, wuq, wdkv, gkv, wkpe, wuk, wuvt, batch, seq):
    n = x.shape[0]
    sps = seq // STEP_ROWS
    hw = MLA_HEADS * HEAD_W
    hv = MLA_HEADS * V_HEAD
    pos_spec = pl.BlockSpec((STEP_ROWS, LANES), lambda i: (i % sps, 0))
    return pl.pallas_call(
        _mla_proj_kernel,
        grid=(n // STEP_ROWS,),
        in_specs=[_row_spec(D_MODEL), pos_spec, pos_spec,
                  _const_spec((D_MODEL, Q_LORA)), _const_spec((1, Q_LORA)),
                  _const_spec((Q_LORA, MLA_HEADS * (QK_NOPE + 2 * QK_ROPE))),
                  _const_spec((D_MODEL, KV_LORA)), _const_spec((1, KV_LORA)),
                  _const_spec((D_MODEL, 2 * LANES)),
                  _const_spec((KV_LORA, MLA_HEADS * QK_NOPE)), _const_spec((hv, KV_LORA))],
        out_specs=[_row_spec(hw), _row_spec(hw),
                   pl.BlockSpec((1, hv, STEP_ROWS), lambda i: (i // sps, 0, i % sps))],
        out_shape=[jax.ShapeDtypeStruct((n, hw), BF16), jax.ShapeDtypeStruct((n, hw), BF16),
                   jax.ShapeDtypeStruct((batch, hv, seq), BF16)],
        compiler_params=_params("parallel"),
        name="mla_proj",
    )(x, cos, sin, wdq, gq, wuq, wdkv, gkv, wkpe, wuk, wuvt)


def _mla_attn_kernel(q_ref, k_ref, vt_ref, *refs, seq, n_cast):
    cast_src, o_ref, cast_dst = refs[:n_cast], refs[n_cast], refs[n_cast + 1:]
    _cast_blocks(cast_src, cast_dst)
    key_chunk = lax.broadcasted_iota(jnp.int32, (ATT_TQ, ATT_TQ), 0) // CHUNK
    qry_chunk = lax.broadcasted_iota(jnp.int32, (ATT_TQ, ATT_TQ), 1) // CHUNK
    diag_allowed = key_chunk <= qry_chunk

    def scores(hd, qi):
        lo, hi = qi * ATT_TQ, (qi + 1) * ATT_TQ
        hcols = slice(hd * HEAD_W, (hd + 1) * HEAD_W)
        q = q_ref[0, lo:hi, hcols]
        s_diag = lax.dot_general(k_ref[0, lo:hi, hcols], q, NT_DIMS, preferred_element_type=F32)
        s_diag = jnp.where(diag_allowed, s_diag, -1e30)
        m = jnp.max(s_diag, axis=0, keepdims=True)
        s_off = None
        if qi > 0:
            s_off = lax.dot_general(k_ref[0, 0:lo, hcols], q, NT_DIMS, preferred_element_type=F32)
            m = jnp.maximum(m, jnp.max(s_off, axis=0, keepdims=True))
        return s_diag, s_off, m

    def finish(hd, qi, s_diag, s_off, m):
        lo, hi = qi * ATT_TQ, (qi + 1) * ATT_TQ
        vrows = slice(hd * V_HEAD, (hd + 1) * V_HEAD)
        p_diag = jnp.exp2(s_diag - m)
        l = jnp.sum(p_diag, axis=0, keepdims=True)
        acc = _dot(vt_ref[0, vrows, lo:hi], p_diag.astype(BF16))
        if qi > 0:
            p_off = jnp.exp2(s_off - m)
            l = l + jnp.sum(p_off, axis=0, keepdims=True)
            acc = acc + _dot(vt_ref[0, vrows, 0:lo], p_off.astype(BF16))
        o_ref[0, vrows, lo:hi] = (acc / l).astype(BF16)

    items = [(hd, qi) for qi in range(seq // ATT_TQ) for hd in range(ATT_HEADS)]
    pending = scores(*items[0])
    for n, item in enumerate(items):
        nxt = scores(*items[n + 1]) if n + 1 < len(items) else None
        finish(*item, *pending)
        pending = nxt


def _mla_attn(q, k, vt, batch, seq, cast=()):
    assert ATT_TQ % CHUNK == 0 and seq % ATT_TQ == 0 and MLA_HEADS % ATT_HEADS == 0
    head_groups = MLA_HEADS // ATT_HEADS
    qk_spec = pl.BlockSpec((1, seq, ATT_HEADS * HEAD_W), lambda b, h: (b, 0, h))
    vt_spec = pl.BlockSpec((1, ATT_HEADS * V_HEAD, seq), lambda b, h: (b, h, 0))
    cast_in, cast_out, cast_shapes = _cast_specs(cast, batch * head_groups, lambda b, h: b * head_groups + h)
    return pl.pallas_call(
        functools.partial(_mla_attn_kernel, seq=seq, n_cast=len(cast)),
        grid=(batch, head_groups),
        in_specs=[qk_spec, qk_spec, vt_spec] + cast_in,
        out_specs=[vt_spec] + cast_out,
        out_shape=[jax.ShapeDtypeStruct((batch, MLA_HEADS * V_HEAD, seq), BF16)] + cast_shapes,
        compiler_params=_params("parallel", "parallel"),
        name="mla_attn",
    )(q.reshape(batch, seq, MLA_HEADS * HEAD_W), k.reshape(batch, seq, MLA_HEADS * HEAD_W), vt,
      *[w for w, _ in cast])


def _rope_tables(seq):
    pos = jnp.arange(seq, dtype=F32)
    inv_freq = ROPE_THETA ** (-jnp.arange(0, QK_ROPE, 2, dtype=F32) / QK_ROPE)
    ang = pos[:, None] * inv_freq[None, :]
    reps = LANES // (QK_ROPE // 2)
    return jnp.tile(jnp.cos(ang), (1, reps)), jnp.tile(jnp.sin(ang), (1, reps))


def _rotate_half_cols(w):
    half = w.shape[-1] // 2
    return jnp.concatenate([-w[..., half:], w[..., :half]], axis=-1)


def _mla_attention(x, w_dq, g_q, w_uq, w_dkv, g_kv, w_uk, w_uv, batch, seq, cast=()):
    wq = w_uq.reshape(Q_LORA, MLA_HEADS, QK_NOPE + QK_ROPE)
    wq_pe = wq[..., QK_NOPE:]
    wuq = jnp.concatenate(
        [wq[..., :QK_NOPE].reshape(Q_LORA, -1), wq_pe.reshape(Q_LORA, -1),
         _rotate_half_cols(wq_pe).reshape(Q_LORA, -1)], axis=1).astype(BF16)
    wk_pe = w_dkv[:, KV_LORA:]
    wk_pe_rot = _rotate_half_cols(wk_pe)
    wkpe = jnp.concatenate([wk_pe, wk_pe, wk_pe_rot, wk_pe_rot], axis=1).astype(BF16)
    wuvt = w_uv.reshape(KV_LORA, MLA_HEADS * V_HEAD).T.astype(BF16)
    cos, sin = _rope_tables(seq)
    q, k, vt = _mla_proj(x, cos, sin, w_dq.astype(BF16), g_q[None], wuq, w_dkv[:, :KV_LORA].astype(BF16),
                         g_kv[None], wkpe, w_uk.reshape(KV_LORA, -1).astype(BF16), wuvt, batch, seq)
    return _mla_attn(q, k, vt, batch, seq, cast=cast)


def kernel(x, sc_w_in, sc_conv_w, sc_w_out, mla_w_dq, mla_g_q, mla_w_uq, mla_w_dkv, mla_g_kv, mla_w_uk, mla_w_uv, mla_w_o, cf_w_pw1, cf_b_pw1, cf_dw_w, cf_dw_b, cf_norm_g, cf_norm_b, cf_w_pw2, cf_b_pw2, ff_w1, ff_w2, ln_mix_g, ln_mix_b, ln_ff_g, ln_ff_b):
    batch, seq, d = x.shape
    assert d == D_MODEL and seq % STEP_ROWS == 0
    h = x.reshape(batch * seq, d)

    calls = []
    for i in range(DEPTH):
        m, j = i % N_MIXERS, i // N_MIXERS
        mlp_w = [(ff_w1, i), (ff_w2, i)]
        if m == 1:
            calls.append(("attention", i, [], True))
            calls.append(("attn_out+mlp", i, mlp_w, False))
        else:
            kind = "sconv" if m == 0 else "conf"
            mixer_w = [(sc_w_in, j), (sc_w_out, j)] if m == 0 else [(cf_w_pw1, j), (cf_w_pw2, j)]
            if i == 0:
                calls.append((kind, i, mixer_w, True))
                calls.append(("mlp", i, mlp_w, True))
            else:
                calls.append((kind + "+mlp", i, mixer_w + mlp_w, False))
    jobs = [[] for _ in calls]
    source = [None] * len(calls)
    carrier = None
    for c, (_, _, needed, can_carry) in enumerate(calls):
        if needed and carrier is not None:
            source[c] = (carrier, len(jobs[carrier]))
            jobs[carrier].extend(needed)
        if can_carry:
            carrier = c
    casted = [None] * len(calls)

    def bf16_weights(c):
        needed = calls[c][2]
        if source[c] is None:
            return [w[layer].astype(BF16) for w, layer in needed]
        carrier, first = source[c]
        return casted[carrier][first:first + len(needed)]

    attn = None
    for c, (kind, i, _, _) in enumerate(calls):
        j = i // N_MIXERS
        g, b = ln_mix_g[i][None], ln_mix_b[i][None]
        weights = bf16_weights(c)
        mixer = {}
        if kind.startswith("sconv"):
            mixer = _sconv_mixer(weights[0], sc_conv_w[j], weights[1], g, b, seq)
        elif kind.startswith("conf"):
            mixer = _conf_mixer(weights[0], cf_b_pw1[j][None], cf_dw_w[j], cf_dw_b[j][None], cf_norm_g[j][None],
                                cf_norm_b[j][None], weights[1], cf_b_pw2[j][None], g, b, seq)
        elif kind.startswith("attn_out"):
            mixer = _attn_out_mixer(attn, mla_w_o[j].astype(BF16), g, b, seq)
        if kind == "attention":
            attn, *casted[c] = _mla_attention(h, mla_w_dq[j], mla_g_q[j], mla_w_uq[j], mla_w_dkv[j], mla_g_kv[j],
                                              mla_w_uk[j], mla_w_uv[j], batch, seq, cast=jobs[c])
            continue
        mlp = (*weights[-2:], ln_ff_g[i][None], ln_ff_b[i][None]) if kind.endswith("mlp") else None
        h, *casted[c] = _sublayer_call(kind.replace("+", "_") + "_ln", h, mlp=mlp, cast=jobs[c], **mixer)
    return h.reshape(batch, seq, d)
```

```python
import functools

import jax
import jax.numpy as jnp
from jax import lax
from jax.experimental import pallas as pl
from jax.experimental.pallas import tpu as pltpu

D_MODEL = 1024
DEPTH = 4
CHUNK = 64
N_MIXERS = 3
ALPHA = (2.0 * DEPTH) ** 0.25
LN_EPS = 1e-5
RMS_EPS = 1e-6
SC_WIDTH = 3
MLA_HEADS = 8
QK_NOPE = 128
QK_ROPE = 64
V_HEAD = 128
Q_LORA = 3 * D_MODEL // 8
KV_LORA = D_MODEL // 4
ROPE_THETA = 10000.0
CONF_WIDTH = 31
D_FF = 4 * D_MODEL

LANES = 128
SUBLANES = 8
N_SLABS = D_MODEL // LANES
HEAD_W = 2 * LANES
VMEM_LIMIT = 60 * 1024 * 1024

SUB_ROWS = 512
SUBTILES = 2
STEP_ROWS = SUBTILES * SUB_ROWS
ATT_TQ = 256
ATT_HEADS = 4
SC_HALO = 8
CF_HALO = 32
CONV_ROWS = 128
FF_CHUNKS = 4

BF16 = jnp.bfloat16
F32 = jnp.float32
LOG2_E = 1.4426950408889634
NT_DIMS = (((1,), (1,)), ((), ()))
TN_DIMS = (((0,), (0,)), ((), ()))


def _dot(a, b):
    return jnp.dot(a, b, preferred_element_type=F32)


def _layer_norm(z, g, b):
    mu = jnp.mean(z, axis=-1, keepdims=True)
    zc = z - mu
    var = jnp.mean(zc * zc, axis=-1, keepdims=True)
    return zc * lax.rsqrt(var + LN_EPS) * g + b


def _rms_norm(z, g):
    return z * lax.rsqrt(jnp.mean(z * z, axis=-1, keepdims=True) + RMS_EPS) * g


def _sub(t):
    return slice(t * SUB_ROWS, (t + 1) * SUB_ROWS)


def _const_spec(shape):
    nd = len(shape)
    return pl.BlockSpec(shape, lambda *_: (0,) * nd, pipeline_mode=pl.Buffered(1))


def _row_spec(width):
    return pl.BlockSpec((STEP_ROWS, width), lambda i: (i, 0))


def _params(*sem):
    return pltpu.CompilerParams(dimension_semantics=sem, vmem_limit_bytes=VMEM_LIMIT)


def _cast_specs(jobs, n_steps, step=lambda i: i):
    in_specs, out_specs, shapes = [], [], []
    for w, layer in jobs:
        _, rows, cols = w.shape
        assert rows % n_steps == 0 and (rows // n_steps) % (2 * SUBLANES) == 0
        in_specs.append(pl.BlockSpec((None, rows // n_steps, cols),
                                     lambda *idx, layer=layer: (layer, step(*idx), 0)))
        out_specs.append(pl.BlockSpec((rows // n_steps, cols), lambda *idx: (step(*idx), 0)))
        shapes.append(jax.ShapeDtypeStruct((rows, cols), BF16))
    return in_specs, out_specs, shapes


def _cast_blocks(src_refs, dst_refs):
    for src, dst in zip(src_refs, dst_refs):
        dst[...] = src[...].astype(BF16)


def _mlp_stage(x_ref, w1_ref, w2_ref, g_ref, b_ref, o_ref):
    fc = D_FF // FF_CHUNKS

    def finish(rows, y):
        o_ref[rows, :] = _layer_norm(ALPHA * x_ref[rows, :] + y, g_ref[...], b_ref[...])

    pending = None
    for t in range(SUBTILES):
        xb = x_ref[_sub(t), :].astype(BF16)
        y = None
        for c in range(FF_CHUNKS):
            h = _dot(xb, w1_ref[:, c * fc:(c + 1) * fc])
            h = jnp.maximum(h, 0.0)
            h = (h * h).astype(BF16)
            yc = _dot(h, w2_ref[c * fc:(c + 1) * fc, :])
            y = yc if y is None else y + yc
            if c == 0 and pending is not None:
                finish(*pending)
        pending = (_sub(t), y)
    finish(*pending)


def _sconv_stage(x_ref, win_ref, cw_ref, wout_ref, g_ref, b_ref, o_ref, ubuf, gbuf, *, steps_per_seq):
    @pl.when(pl.program_id(0) % steps_per_seq == 0)
    def _():
        ubuf[:, 0:SC_HALO, :] = jnp.zeros((N_SLABS, SC_HALO, LANES), F32)

    b_gates = []
    for t in range(SUBTILES):
        xb = x_ref[_sub(t), :].astype(BF16)
        b_gates.append(_dot(xb, win_ref[:, 0:D_MODEL]))
        u = _dot(xb, win_ref[:, D_MODEL:2 * D_MODEL]) * _dot(xb, win_ref[:, 2 * D_MODEL:3 * D_MODEL])
        for s in range(N_SLABS):
            ubuf[s, SC_HALO + t * SUB_ROWS:SC_HALO + (t + 1) * SUB_ROWS, :] = u[:, s * LANES:(s + 1) * LANES]

    ys = []
    for t in range(SUBTILES):
        for s in range(N_SLABS):
            cols = slice(s * LANES, (s + 1) * LANES)
            conv = None
            for k in range(SC_WIDTH):
                start = SC_HALO + t * SUB_ROWS - (SC_WIDTH - 1) + k
                term = cw_ref[k:k + 1, cols] * ubuf[s, start:start + SUB_ROWS, :]
                conv = term if conv is None else conv + term
            gbuf[_sub(t), cols] = (b_gates[t][:, cols] * conv).astype(BF16)
        ys.append(_dot(gbuf[_sub(t), :], wout_ref[...]))

    for t in range(SUBTILES):
        o_ref[_sub(t), :] = _layer_norm(ALPHA * x_ref[_sub(t), :] + ys[t], g_ref[...], b_ref[...])

    ubuf[:, 0:SC_HALO, :] = ubuf[:, STEP_ROWS:STEP_ROWS + SC_HALO, :]


def _conf_stage(x_ref, w1_ref, b1_ref, dww_ref, dwb_ref, ng_ref, nb_ref, w2_ref, b2_ref, g_ref, b_ref,
                o_ref, hbuf, cbuf, *, steps_per_seq):
    @pl.when(pl.program_id(0) % steps_per_seq == 0)
    def _():
        hbuf[:, 0:CF_HALO, :] = jnp.zeros((N_SLABS, CF_HALO, LANES), F32)

    for t in range(SUBTILES):
        xb = x_ref[_sub(t), :].astype(BF16)
        a = _dot(xb, w1_ref[:, 0:D_MODEL]) + b1_ref[:, 0:D_MODEL]
        gate = _dot(xb, w1_ref[:, D_MODEL:2 * D_MODEL]) + b1_ref[:, D_MODEL:2 * D_MODEL]
        h = a * jax.nn.sigmoid(gate)
        for s in range(N_SLABS):
            hbuf[s, CF_HALO + t * SUB_ROWS:CF_HALO + (t + 1) * SUB_ROWS, :] = h[:, s * LANES:(s + 1) * LANES]

    for t in range(SUBTILES):
        for s in range(N_SLABS):
            cols = slice(s * LANES, (s + 1) * LANES)
            for r0 in range(t * SUB_ROWS, (t + 1) * SUB_ROWS, CONV_ROWS):
                acc = jnp.zeros((CONV_ROWS, LANES), F32) + dwb_ref[:, cols]
                for k in range(CONF_WIDTH):
                    start = CF_HALO + r0 - (CONF_WIDTH - 1) + k
                    acc = acc + dww_ref[k:k + 1, cols] * hbuf[s, start:start + CONV_ROWS, :]
                cbuf[r0:r0 + CONV_ROWS, cols] = acc
        hn = _layer_norm(cbuf[_sub(t), :], ng_ref[...], nb_ref[...])
        hs = (hn * jax.nn.sigmoid(hn)).astype(BF16)
        y = _dot(hs, w2_ref[...]) + b2_ref[...]
        o_ref[_sub(t), :] = _layer_norm(ALPHA * x_ref[_sub(t), :] + y, g_ref[...], b_ref[...])

    hbuf[:, 0:CF_HALO, :] = hbuf[:, STEP_ROWS:STEP_ROWS + CF_HALO, :]


def _attn_out_stage(x_ref, ot_ref, wo_ref, g_ref, b_ref, o_ref):
    ys = [lax.dot_general(ot_ref[0, :, _sub(t)], wo_ref[...], TN_DIMS, preferred_element_type=F32)
          for t in range(SUBTILES)]
    for t in range(SUBTILES):
        o_ref[_sub(t), :] = _layer_norm(ALPHA * x_ref[_sub(t), :] + ys[t], g_ref[...], b_ref[...])


def _sublayer_kernel(x_ref, *refs, mixer, n_mixer_in, with_mlp, n_cast):
    mixer_in, refs = refs[:n_mixer_in], refs[n_mixer_in:]
    n_mlp_in = 4 if with_mlp else 0
    mlp_in, refs = refs[:n_mlp_in], refs[n_mlp_in:]
    cast_src, o_ref, cast_dst, scratch = refs[:n_cast], refs[n_cast], refs[n_cast + 1:2 * n_cast + 1], refs[2 * n_cast + 1:]
    _cast_blocks(cast_src, cast_dst)
    if mixer is not None:
        mixer(x_ref, *mixer_in, o_ref, *scratch)
    if with_mlp:
        _mlp_stage(x_ref if mixer is None else o_ref, *mlp_in, o_ref)


def _sublayer_call(name, x, mixer=None, mixer_args=(), mixer_specs=(), scratch=(), mlp=None, cast=()):
    n = x.shape[0]
    n_steps = n // STEP_ROWS
    vec = _const_spec((1, D_MODEL))
    mlp_specs = [] if mlp is None else [_const_spec((D_MODEL, D_FF)), _const_spec((D_FF, D_MODEL)), vec, vec]
    cast_in, cast_out, cast_shapes = _cast_specs(cast, n_steps)
    return pl.pallas_call(
        functools.partial(_sublayer_kernel, mixer=mixer, n_mixer_in=len(mixer_args),
                          with_mlp=mlp is not None, n_cast=len(cast)),
        grid=(n_steps,),
        in_specs=[_row_spec(D_MODEL)] + list(mixer_specs) + mlp_specs + cast_in,
        out_specs=[_row_spec(D_MODEL)] + cast_out,
        out_shape=[jax.ShapeDtypeStruct((n, D_MODEL), F32)] + cast_shapes,
        scratch_shapes=list(scratch),
        compiler_params=_params("parallel" if mixer is None else "arbitrary"),
        name=name,
    )(x, *mixer_args, *(mlp or ()), *[w for w, _ in cast])


def _sconv_mixer(w_in, conv_w, w_out, g, b, seq):
    vec = _const_spec((1, D_MODEL))
    return dict(
        mixer=functools.partial(_sconv_stage, steps_per_seq=seq // STEP_ROWS),
        mixer_args=(w_in, conv_w, w_out, g, b),
        mixer_specs=(_const_spec((D_MODEL, 3 * D_MODEL)), _const_spec((SC_WIDTH, D_MODEL)),
                     _const_spec((D_MODEL, D_MODEL)), vec, vec),
        scratch=(pltpu.VMEM((N_SLABS, SC_HALO + STEP_ROWS, LANES), F32), pltpu.VMEM((STEP_ROWS, D_MODEL), BF16)))


def _conf_mixer(w_pw1, b_pw1, dw_w, dw_b, norm_g, norm_b, w_pw2, b_pw2, g, b, seq):
    vec = _const_spec((1, D_MODEL))
    return dict(
        mixer=functools.partial(_conf_stage, steps_per_seq=seq // STEP_ROWS),
        mixer_args=(w_pw1, b_pw1, dw_w, dw_b, norm_g, norm_b, w_pw2, b_pw2, g, b),
        mixer_specs=(_const_spec((D_MODEL, 2 * D_MODEL)), _const_spec((1, 2 * D_MODEL)),
                     _const_spec((CONF_WIDTH, D_MODEL)), vec, vec, vec, _const_spec((D_MODEL, D_MODEL)), vec, vec, vec),
        scratch=(pltpu.VMEM((N_SLABS, CF_HALO + STEP_ROWS, LANES), F32), pltpu.VMEM((STEP_ROWS, D_MODEL), F32)))


def _attn_out_mixer(ot, w_o, g, b, seq):
    sps = seq // STEP_ROWS
    hv = MLA_HEADS * V_HEAD
    vec = _const_spec((1, D_MODEL))
    return dict(
        mixer=_attn_out_stage,
        mixer_args=(ot, w_o, g, b),
        mixer_specs=(pl.BlockSpec((1, hv, STEP_ROWS), lambda i: (i // sps, 0, i % sps)),
                     _const_spec((hv, D_MODEL)), vec, vec))


def _mla_proj_kernel(x_ref, cos_ref, sin_ref, wdq_ref, gq_ref, wuq_ref, wdkv_ref, gkv_ref, wkpe_ref,
                     wuk_ref, wuvt_ref, q_ref, k_ref, vt_ref):
    nope_w = MLA_HEADS * QK_NOPE
    rope_w = MLA_HEADS * QK_ROPE
    scale = (QK_NOPE + QK_ROPE) ** -0.5 * LOG2_E
    first_half = lax.broadcasted_iota(jnp.int32, (SUB_ROWS, LANES), 1) < QK_ROPE
    for t in range(SUBTILES):
        xb = x_ref[_sub(t), :].astype(BF16)
        cos = cos_ref[_sub(t), :]
        sin = sin_ref[_sub(t), :]

        cq = _rms_norm(_dot(xb, wdq_ref[...]), gq_ref[...]).astype(BF16)
        qall = _dot(cq, wuq_ref[...])
        for hd in range(MLA_HEADS):
            q_ref[_sub(t), hd * HEAD_W:hd * HEAD_W + LANES] = (
                qall[:, hd * QK_NOPE:(hd + 1) * QK_NOPE] * scale).astype(BF16)
        for pair in range(MLA_HEADS // 2):
            pe = qall[:, nope_w + pair * LANES:nope_w + (pair + 1) * LANES]
            pe_rot = qall[:, nope_w + rope_w + pair * LANES:nope_w + rope_w + (pair + 1) * LANES]
            slab = ((pe * cos + pe_rot * sin) * scale).astype(BF16)
            for hd in (2 * pair, 2 * pair + 1):
                q_ref[_sub(t), hd * HEAD_W + LANES:(hd + 1) * HEAD_W] = slab

        ckv = _rms_norm(_dot(xb, wdkv_ref[...]), gkv_ref[...]).astype(BF16)
        kpe2 = _dot(xb, wkpe_ref[...])
        k_pe = kpe2[:, 0:LANES] * cos + kpe2[:, LANES:2 * LANES] * sin
        k_pe_even = jnp.where(first_half, k_pe, 0.0).astype(BF16)
        k_pe_odd = jnp.where(first_half, 0.0, k_pe).astype(BF16)
        k_nope = _dot(ckv, wuk_ref[...])
        for hd in range(MLA_HEADS):
            k_ref[_sub(t), hd * HEAD_W:hd * HEAD_W + LANES] = (
                k_nope[:, hd * QK_NOPE:(hd + 1) * QK_NOPE].astype(BF16))
            k_ref[_sub(t), hd * HEAD_W + LANES:(hd + 1) * HEAD_W] = k_pe_even if hd % 2 == 0 else k_pe_odd
        vt = lax.dot_general(wuvt_ref[...], ckv, NT_DIMS, preferred_element_type=F32)
        vt_ref[0, :, _sub(t)] = vt.astype(BF16)


def _mla_proj(x, cos, sin, wdq, gq, wuq, wdkv, gkv, wkpe, wuk, wuvt, batch, seq):
    n = x.shape[0]
    sps = seq // STEP_ROWS
    hw = MLA_HEADS * HEAD_W
    hv = MLA_HEADS * V_HEAD
    pos_spec = pl.BlockSpec((STEP_ROWS, LANES), lambda i: (i % sps, 0))
    return pl.pallas_call(
        _mla_proj_kernel,
        grid=(n // STEP_ROWS,),
        in_specs=[_row_spec(D_MODEL), pos_spec, pos_spec,
                  _const_spec((D_MODEL, Q_LORA)), _const_spec((1, Q_LORA)),
                  _const_spec((Q_LORA, MLA_HEADS * (QK_NOPE + 2 * QK_ROPE))),
                  _const_spec((D_MODEL, KV_LORA)), _const_spec((1, KV_LORA)),
                  _const_spec((D_MODEL, 2 * LANES)),
                  _const_spec((KV_LORA, MLA_HEADS * QK_NOPE)), _const_spec((hv, KV_LORA))],
        out_specs=[_row_spec(hw), _row_spec(hw),
                   pl.BlockSpec((1, hv, STEP_ROWS), lambda i: (i // sps, 0, i % sps))],
        out_shape=[jax.ShapeDtypeStruct((n, hw), BF16), jax.ShapeDtypeStruct((n, hw), BF16),
                   jax.ShapeDtypeStruct((batch, hv, seq), BF16)],
        compiler_params=_params("parallel"),
        name="mla_proj",
    )(x, cos, sin, wdq, gq, wuq, wdkv, gkv, wkpe, wuk, wuvt)


def _mla_attn_kernel(q_ref, k_ref, vt_ref, *refs, seq, n_cast):
    cast_src, o_ref, cast_dst = refs[:n_cast], refs[n_cast], refs[n_cast + 1:]
    _cast_blocks(cast_src, cast_dst)
    key_chunk = lax.broadcasted_iota(jnp.int32, (ATT_TQ, ATT_TQ), 0) // CHUNK
    qry_chunk = lax.broadcasted_iota(jnp.int32, (ATT_TQ, ATT_TQ), 1) // CHUNK
    diag_allowed = key_chunk <= qry_chunk

    def scores(hd, qi):
        lo, hi = qi * ATT_TQ, (qi + 1) * ATT_TQ
        hcols = slice(hd * HEAD_W, (hd + 1) * HEAD_W)
        q = q_ref[0, lo:hi, hcols]
        s_diag = lax.dot_general(k_ref[0, lo:hi, hcols], q, NT_DIMS, preferred_element_type=F32)
        s_diag = jnp.where(diag_allowed, s_diag, -1e30)
        m = jnp.max(s_diag, axis=0, keepdims=True)
        s_off = None
        if qi > 0:
            s_off = lax.dot_general(k_ref[0, 0:lo, hcols], q, NT_DIMS, preferred_element_type=F32)
            m = jnp.maximum(m, jnp.max(s_off, axis=0, keepdims=True))
        return s_diag, s_off, m

    def finish(hd, qi, s_diag, s_off, m):
        lo, hi = qi * ATT_TQ, (qi + 1) * ATT_TQ
        vrows = slice(hd * V_HEAD, (hd + 1) * V_HEAD)
        p_diag = jnp.exp2(s_diag - m)
        l = jnp.sum(p_diag, axis=0, keepdims=True)
        acc = _dot(vt_ref[0, vrows, lo:hi], p_diag.astype(BF16))
        if qi > 0:
            p_off = jnp.exp2(s_off - m)
            l = l + jnp.sum(p_off, axis=0, keepdims=True)
            acc = acc + _dot(vt_ref[0, vrows, 0:lo], p_off.astype(BF16))
        o_ref[0, vrows, lo:hi] = (acc / l).astype(BF16)

    items = [(hd, qi) for qi in range(seq // ATT_TQ) for hd in range(ATT_HEADS)]
    pending = scores(*items[0])
    for n, item in enumerate(items):
        nxt = scores(*items[n + 1]) if n + 1 < len(items) else None
        finish(*item, *pending)
        pending = nxt


def _mla_attn(q, k, vt, batch, seq, cast=()):
    assert ATT_TQ % CHUNK == 0 and seq % ATT_TQ == 0 and MLA_HEADS % ATT_HEADS == 0
    head_groups = MLA_HEADS // ATT_HEADS
    qk_spec = pl.BlockSpec((1, seq, ATT_HEADS * HEAD_W), lambda b, h: (b, 0, h))
    vt_spec = pl.BlockSpec((1, ATT_HEADS * V_HEAD, seq), lambda b, h: (b, h, 0))
    cast_in, cast_out, cast_shapes = _cast_specs(cast, batch * head_groups, lambda b, h: b * head_groups + h)
    return pl.pallas_call(
        functools.partial(_mla_attn_kernel, seq=seq, n_cast=len(cast)),
        grid=(batch, head_groups),
        in_specs=[qk_spec, qk_spec, vt_spec] + cast_in,
        out_specs=[vt_spec] + cast_out,
        out_shape=[jax.ShapeDtypeStruct((batch, MLA_HEADS * V_HEAD, seq), BF16)] + cast_shapes,
        compiler_params=_params("parallel", "parallel"),
        name="mla_attn",
    )(q.reshape(batch, seq, MLA_HEADS * HEAD_W), k.reshape(batch, seq, MLA_HEADS * HEAD_W), vt,
      *[w for w, _ in cast])


def _rope_tables(seq):
    pos = jnp.arange(seq, dtype=F32)
    inv_freq = ROPE_THETA ** (-jnp.arange(0, QK_ROPE, 2, dtype=F32) / QK_ROPE)
    ang = pos[:, None] * inv_freq[None, :]
    reps = LANES // (QK_ROPE // 2)
    return jnp.tile(jnp.cos(ang), (1, reps)), jnp.tile(jnp.sin(ang), (1, reps))


def _rotate_half_cols(w):
    half = w.shape[-1] // 2
    return jnp.concatenate([-w[..., half:], w[..., :half]], axis=-1)


def _mla_attention(x, w_dq, g_q, w_uq, w_dkv, g_kv, w_uk, w_uv, batch, seq, cast=()):
    wq = w_uq.reshape(Q_LORA, MLA_HEADS, QK_NOPE + QK_ROPE)
    wq_pe = wq[..., QK_NOPE:]
    wuq = jnp.concatenate(
        [wq[..., :QK_NOPE].reshape(Q_LORA, -1), wq_pe.reshape(Q_LORA, -1),
         _rotate_half_cols(wq_pe).reshape(Q_LORA, -1)], axis=1).astype(BF16)
    wk_pe = w_dkv[:, KV_LORA:]
    wk_pe_rot = _rotate_half_cols(wk_pe)
    wkpe = jnp.concatenate([wk_pe, wk_pe, wk_pe_rot, wk_pe_rot], axis=1).astype(BF16)
    wuvt = w_uv.reshape(KV_LORA, MLA_HEADS * V_HEAD).T.astype(BF16)
    cos, sin = _rope_tables(seq)
    q, k, vt = _mla_proj(x, cos, sin, w_dq.astype(BF16), g_q[None], wuq, w_dkv[:, :KV_LORA].astype(BF16),
                         g_kv[None], wkpe, w_uk.reshape(KV_LORA, -1).astype(BF16), wuvt, batch, seq)
    return _mla_attn(q, k, vt, batch, seq, cast=cast)


def kernel(x, sc_w_in, sc_conv_w, sc_w_out, mla_w_dq, mla_g_q, mla_w_uq, mla_w_dkv, mla_g_kv, mla_w_uk, mla_w_uv, mla_w_o, cf_w_pw1, cf_b_pw1, cf_dw_w, cf_dw_b, cf_norm_g, cf_norm_b, cf_w_pw2, cf_b_pw2, ff_w1, ff_w2, ln_mix_g, ln_mix_b, ln_ff_g, ln_ff_b):
    batch, seq, d = x.shape
    assert d == D_MODEL and seq % STEP_ROWS == 0
    h = x.reshape(batch * seq, d)

    calls = []
    for i in range(DEPTH):
        m, j = i % N_MIXERS, i // N_MIXERS
        mlp_w = [(ff_w1, i), (ff_w2, i)]
        if m == 1:
            calls.append(("attention", i, [], True))
            calls.append(("attn_out+mlp", i, mlp_w, False))
        elif m == 0 and i > 0:
            calls.append(("sconv+mlp", i, [(sc_w_in, j), (sc_w_out, j)] + mlp_w, False))
        else:
            mixer_w = [(sc_w_in, j), (sc_w_out, j)] if m == 0 else [(cf_w_pw1, j), (cf_w_pw2, j)]
            calls.append(("sconv" if m == 0 else "conf", i, mixer_w, True))
            calls.append(("mlp", i, mlp_w, True))
    jobs = [[] for _ in calls]
    source = [None] * len(calls)
    carrier = None
    for c, (_, _, needed, can_carry) in enumerate(calls):
        if needed and carrier is not None:
            source[c] = (carrier, len(jobs[carrier]))
            jobs[carrier].extend(needed)
        if can_carry:
            carrier = c
    casted = [None] * len(calls)

    def bf16_weights(c):
        needed = calls[c][2]
        if source[c] is None:
            return [w[layer].astype(BF16) for w, layer in needed]
        carrier, first = source[c]
        return casted[carrier][first:first + len(needed)]

    attn = None
    for c, (kind, i, _, _) in enumerate(calls):
        j = i // N_MIXERS
        g, b = ln_mix_g[i][None], ln_mix_b[i][None]
        weights = bf16_weights(c)
        mixer = {}
        if kind.startswith("sconv"):
            mixer = _sconv_mixer(weights[0], sc_conv_w[j], weights[1], g, b, seq)
        elif kind.startswith("conf"):
            mixer = _conf_mixer(weights[0], cf_b_pw1[j][None], cf_dw_w[j], cf_dw_b[j][None], cf_norm_g[j][None],
                                cf_norm_b[j][None], weights[1], cf_b_pw2[j][None], g, b, seq)
        elif kind.startswith("attn_out"):
            mixer = _attn_out_mixer(attn, mla_w_o[j].astype(BF16), g, b, seq)
        if kind == "attention":
            attn, *casted[c] = _mla_attention(h, mla_w_dq[j], mla_g_q[j], mla_w_uq[j], mla_w_dkv[j], mla_g_kv[j],
                                              mla_w_uk[j], mla_w_uv[j], batch, seq, cast=jobs[c])
            continue
        mlp = (*weights[-2:], ln_ff_g[i][None], ln_ff_b[i][None]) if kind.endswith("mlp") else None
        h, *casted[c] = _sublayer_call(kind.replace("+", "_") + "_ln", h, mlp=mlp, cast=jobs[c], **mixer)
    return h.reshape(batch, seq, d)
```

```python
import functools

import jax
import jax.numpy as jnp
from jax import lax
from jax.experimental import pallas as pl
from jax.experimental.pallas import tpu as pltpu

D_MODEL = 1024
DEPTH = 4
CHUNK = 64
N_MIXERS = 3
ALPHA = (2.0 * DEPTH) ** 0.25
LN_EPS = 1e-5
RMS_EPS = 1e-6
SC_WIDTH = 3
MLA_HEADS = 8
QK_NOPE = 128
QK_ROPE = 64
V_HEAD = 128
Q_LORA = 3 * D_MODEL // 8
KV_LORA = D_MODEL // 4
ROPE_THETA = 10000.0
CONF_WIDTH = 31
D_FF = 4 * D_MODEL

LANES = 128
SUBLANES = 8
N_SLABS = D_MODEL // LANES
HEAD_W = 2 * LANES
VMEM_LIMIT = 60 * 1024 * 1024

SUB_ROWS = 512
SUBTILES = 2
STEP_ROWS = SUBTILES * SUB_ROWS
ATT_TQ = 256
ATT_HEADS = 4
SC_HALO = 8
CF_HALO = 32
CONV_ROWS = 128
FF_CHUNKS = 4
IN_CHUNK = 256

BF16 = jnp.bfloat16
F32 = jnp.float32
LOG2_E = 1.4426950408889634
NT_DIMS = (((1,), (1,)), ((), ()))
TN_DIMS = (((0,), (0,)), ((), ()))


def _dot(a, b):
    return jnp.dot(a, b, preferred_element_type=F32)


def _layer_norm(z, g, b):
    mu = jnp.mean(z, axis=-1, keepdims=True)
    zc = z - mu
    var = jnp.mean(zc * zc, axis=-1, keepdims=True)
    return zc * lax.rsqrt(var + LN_EPS) * g + b


def _rms_norm(z, g):
    return z * lax.rsqrt(jnp.mean(z * z, axis=-1, keepdims=True) + RMS_EPS) * g


def _sub(t):
    return slice(t * SUB_ROWS, (t + 1) * SUB_ROWS)


def _const_spec(shape):
    nd = len(shape)
    return pl.BlockSpec(shape, lambda *_: (0,) * nd, pipeline_mode=pl.Buffered(1))


def _row_spec(width):
    return pl.BlockSpec((STEP_ROWS, width), lambda i: (i, 0))


def _params(*sem):
    return pltpu.CompilerParams(dimension_semantics=sem, vmem_limit_bytes=VMEM_LIMIT)


def _cast_specs(jobs, n_steps, step=lambda i: i):
    in_specs, out_specs, shapes = [], [], []
    for w, layer in jobs:
        _, rows, cols = w.shape
        assert rows % n_steps == 0 and (rows // n_steps) % (2 * SUBLANES) == 0
        in_specs.append(pl.BlockSpec((None, rows // n_steps, cols),
                                     lambda *idx, layer=layer: (layer, step(*idx), 0)))
        out_specs.append(pl.BlockSpec((rows // n_steps, cols), lambda *idx: (step(*idx), 0)))
        shapes.append(jax.ShapeDtypeStruct((rows, cols), BF16))
    return in_specs, out_specs, shapes


def _cast_blocks(src_refs, dst_refs):
    for src, dst in zip(src_refs, dst_refs):
        dst[...] = src[...].astype(BF16)


def _mlp_stage(x_ref, w1_ref, w2_ref, g_ref, b_ref, o_ref):
    fc = D_FF // FF_CHUNKS

    def finish(rows, y):
        o_ref[rows, :] = _layer_norm(ALPHA * x_ref[rows, :] + y, g_ref[...], b_ref[...])

    pending = None
    for t in range(SUBTILES):
        xb = x_ref[_sub(t), :].astype(BF16)
        y = None
        for c in range(FF_CHUNKS):
            h = _dot(xb, w1_ref[:, c * fc:(c + 1) * fc])
            h = jnp.maximum(h, 0.0)
            h = (h * h).astype(BF16)
            yc = _dot(h, w2_ref[c * fc:(c + 1) * fc, :])
            y = yc if y is None else y + yc
            if c == 0 and pending is not None:
                finish(*pending)
        pending = (_sub(t), y)
    finish(*pending)


def _sconv_stage(x_ref, win_ref, cw_ref, wout_ref, g_ref, b_ref, o_ref, ubuf, gbuf, *, steps_per_seq):
    @pl.when(pl.program_id(0) % steps_per_seq == 0)
    def _():
        ubuf[:, 0:SC_HALO, :] = jnp.zeros((N_SLABS, SC_HALO, LANES), F32)

    slabs_per_chunk = IN_CHUNK // LANES
    b_gates = []
    for t in range(SUBTILES):
        xb = x_ref[_sub(t), :].astype(BF16)
        slabs = []
        for c in range(D_MODEL // IN_CHUNK):
            b_cols, c_cols, h_cols = (slice(part * D_MODEL + c * IN_CHUNK, part * D_MODEL + (c + 1) * IN_CHUNK)
                                      for part in range(3))
            b_gate = _dot(xb, win_ref[:, b_cols])
            u = _dot(xb, win_ref[:, c_cols]) * _dot(xb, win_ref[:, h_cols])
            for i in range(slabs_per_chunk):
                lanes = slice(i * LANES, (i + 1) * LANES)
                slabs.append(b_gate[:, lanes])
                ubuf[c * slabs_per_chunk + i, SC_HALO + t * SUB_ROWS:SC_HALO + (t + 1) * SUB_ROWS, :] = u[:, lanes]
        b_gates.append(slabs)

    ys = []
    for t in range(SUBTILES):
        for s in range(N_SLABS):
            cols = slice(s * LANES, (s + 1) * LANES)
            conv = None
            for k in range(SC_WIDTH):
                start = SC_HALO + t * SUB_ROWS - (SC_WIDTH - 1) + k
                term = cw_ref[k:k + 1, cols] * ubuf[s, start:start + SUB_ROWS, :]
                conv = term if conv is None else conv + term
            gbuf[_sub(t), cols] = (b_gates[t][s] * conv).astype(BF16)
        ys.append(_dot(gbuf[_sub(t), :], wout_ref[...]))

    for t in range(SUBTILES):
        o_ref[_sub(t), :] = _layer_norm(ALPHA * x_ref[_sub(t), :] + ys[t], g_ref[...], b_ref[...])

    ubuf[:, 0:SC_HALO, :] = ubuf[:, STEP_ROWS:STEP_ROWS + SC_HALO, :]


def _conf_stage(x_ref, w1_ref, b1_ref, dww_ref, dwb_ref, ng_ref, nb_ref, w2_ref, b2_ref, g_ref, b_ref,
                o_ref, hbuf, cbuf, *, steps_per_seq):
    @pl.when(pl.program_id(0) % steps_per_seq == 0)
    def _():
        hbuf[:, 0:CF_HALO, :] = jnp.zeros((N_SLABS, CF_HALO, LANES), F32)

    slabs_per_chunk = IN_CHUNK // LANES
    for t in range(SUBTILES):
        xb = x_ref[_sub(t), :].astype(BF16)
        for c in range(D_MODEL // IN_CHUNK):
            a_cols = slice(c * IN_CHUNK, (c + 1) * IN_CHUNK)
            g_cols = slice(D_MODEL + c * IN_CHUNK, D_MODEL + (c + 1) * IN_CHUNK)
            a = _dot(xb, w1_ref[:, a_cols]) + b1_ref[:, a_cols]
            gate = _dot(xb, w1_ref[:, g_cols]) + b1_ref[:, g_cols]
            h = a * jax.nn.sigmoid(gate)
            for i in range(slabs_per_chunk):
                hbuf[c * slabs_per_chunk + i, CF_HALO + t * SUB_ROWS:CF_HALO + (t + 1) * SUB_ROWS, :] = (
                    h[:, i * LANES:(i + 1) * LANES])

    for t in range(SUBTILES):
        for s in range(N_SLABS):
            cols = slice(s * LANES, (s + 1) * LANES)
            for r0 in range(t * SUB_ROWS, (t + 1) * SUB_ROWS, CONV_ROWS):
                acc = jnp.zeros((CONV_ROWS, LANES), F32) + dwb_ref[:, cols]
                for k in range(CONF_WIDTH):
                    start = CF_HALO + r0 - (CONF_WIDTH - 1) + k
                    acc = acc + dww_ref[k:k + 1, cols] * hbuf[s, start:start + CONV_ROWS, :]
                cbuf[r0:r0 + CONV_ROWS, cols] = acc
        hn = _layer_norm(cbuf[_sub(t), :], ng_ref[...], nb_ref[...])
        hs = (hn * jax.nn.sigmoid(hn)).astype(BF16)
        y = _dot(hs, w2_ref[...]) + b2_ref[...]
        o_ref[_sub(t), :] = _layer_norm(ALPHA * x_ref[_sub(t), :] + y, g_ref[...], b_ref[...])

    hbuf[:, 0:CF_HALO, :] = hbuf[:, STEP_ROWS:STEP_ROWS + CF_HALO, :]


def _attn_out_stage(x_ref, ot_ref, wo_ref, g_ref, b_ref, o_ref):
    ys = [lax.dot_general(ot_ref[0, :, _sub(t)], wo_ref[...], TN_DIMS, preferred_element_type=F32)
          for t in range(SUBTILES)]
    for t in range(SUBTILES):
        o_ref[_sub(t), :] = _layer_norm(ALPHA * x_ref[_sub(t), :] + ys[t], g_ref[...], b_ref[...])


def _sublayer_kernel(x_ref, *refs, mixer, n_mixer_in, with_mlp, n_cast):
    mixer_in, refs = refs[:n_mixer_in], refs[n_mixer_in:]
    n_mlp_in = 4 if with_mlp else 0
    mlp_in, refs = refs[:n_mlp_in], refs[n_mlp_in:]
    cast_src, o_ref, cast_dst, scratch = refs[:n_cast], refs[n_cast], refs[n_cast + 1:2 * n_cast + 1], refs[2 * n_cast + 1:]
    _cast_blocks(cast_src, cast_dst)
    if mixer is not None:
        mixer(x_ref, *mixer_in, o_ref, *scratch)
    if with_mlp:
        _mlp_stage(x_ref if mixer is None else o_ref, *mlp_in, o_ref)


def _sublayer_call(name, x, mixer=None, mixer_args=(), mixer_specs=(), scratch=(), mlp=None, cast=()):
    n = x.shape[0]
    n_steps = n // STEP_ROWS
    vec = _const_spec((1, D_MODEL))
    mlp_specs = [] if mlp is None else [_const_spec((D_MODEL, D_FF)), _const_spec((D_FF, D_MODEL)), vec, vec]
    cast_in, cast_out, cast_shapes = _cast_specs(cast, n_steps)
    return pl.pallas_call(
        functools.partial(_sublayer_kernel, mixer=mixer, n_mixer_in=len(mixer_args),
                          with_mlp=mlp is not None, n_cast=len(cast)),
        grid=(n_steps,),
        in_specs=[_row_spec(D_MODEL)] + list(mixer_specs) + mlp_specs + cast_in,
        out_specs=[_row_spec(D_MODEL)] + cast_out,
        out_shape=[jax.ShapeDtypeStruct((n, D_MODEL), F32)] + cast_shapes,
        scratch_shapes=list(scratch),
        compiler_params=_params("parallel" if mixer is None else "arbitrary"),
        name=name,
    )(x, *mixer_args, *(mlp or ()), *[w for w, _ in cast])


def _sconv_mixer(w_in, conv_w, w_out, g, b, seq):
    vec = _const_spec((1, D_MODEL))
    return dict(
        mixer=functools.partial(_sconv_stage, steps_per_seq=seq // STEP_ROWS),
        mixer_args=(w_in, conv_w, w_out, g, b),
        mixer_specs=(_const_spec((D_MODEL, 3 * D_MODEL)), _const_spec((SC_WIDTH, D_MODEL)),
                     _const_spec((D_MODEL, D_MODEL)), vec, vec),
        scratch=(pltpu.VMEM((N_SLABS, SC_HALO + STEP_ROWS, LANES), F32), pltpu.VMEM((STEP_ROWS, D_MODEL), BF16)))


def _conf_mixer(w_pw1, b_pw1, dw_w, dw_b, norm_g, norm_b, w_pw2, b_pw2, g, b, seq):
    vec = _const_spec((1, D_MODEL))
    return dict(
        mixer=functools.partial(_conf_stage, steps_per_seq=seq // STEP_ROWS),
        mixer_args=(w_pw1, b_pw1, dw_w, dw_b, norm_g, norm_b, w_pw2, b_pw2, g, b),
        mixer_specs=(_const_spec((D_MODEL, 2 * D_MODEL)), _const_spec((1, 2 * D_MODEL)),
                     _const_spec((CONF_WIDTH, D_MODEL)), vec, vec, vec, _const_spec((D_MODEL, D_MODEL)), vec, vec, vec),
        scratch=(pltpu.VMEM((N_SLABS, CF_HALO + STEP_ROWS, LANES), F32), pltpu.VMEM((STEP_ROWS, D_MODEL), F32)))


def _attn_out_mixer(ot, w_o, g, b, seq):
    sps = seq // STEP_ROWS
    hv = MLA_HEADS * V_HEAD
    vec = _const_spec((1, D_MODEL))
    return dict(
        mixer=_attn_out_stage,
        mixer_args=(ot, w_o, g, b),
        mixer_specs=(pl.BlockSpec((1, hv, STEP_ROWS), lambda i: (i // sps, 0, i % sps)),
                     _const_spec((hv, D_MODEL)), vec, vec))


def _mla_proj_kernel(x_ref, cos_ref, sin_ref, wdq_ref, gq_ref, wuq_ref, wdkv_ref, gkv_ref, wkpe_ref,
                     wuk_ref, wuvt_ref, q_ref, k_ref, vt_ref):
    nope_w = MLA_HEADS * QK_NOPE
    rope_w = MLA_HEADS * QK_ROPE
    scale = (QK_NOPE + QK_ROPE) ** -0.5 * LOG2_E
    first_half = lax.broadcasted_iota(jnp.int32, (SUB_ROWS, LANES), 1) < QK_ROPE
    for t in range(SUBTILES):
        xb = x_ref[_sub(t), :].astype(BF16)
        cos = cos_ref[_sub(t), :]
        sin = sin_ref[_sub(t), :]

        cq = _rms_norm(_dot(xb, wdq_ref[...]), gq_ref[...]).astype(BF16)
        qall = _dot(cq, wuq_ref[...])
        for hd in range(MLA_HEADS):
            q_ref[_sub(t), hd * HEAD_W:hd * HEAD_W + LANES] = (
                qall[:, hd * QK_NOPE:(hd + 1) * QK_NOPE] * scale).astype(BF16)
        for pair in range(MLA_HEADS // 2):
            pe = qall[:, nope_w + pair * LANES:nope_w + (pair + 1) * LANES]
            pe_rot = qall[:, nope_w + rope_w + pair * LANES:nope_w + rope_w + (pair + 1) * LANES]
            slab = ((pe * cos + pe_rot * sin) * scale).astype(BF16)
            for hd in (2 * pair, 2 * pair + 1):
                q_ref[_sub(t), hd * HEAD_W + LANES:(hd + 1) * HEAD_W] = slab

        ckv = _rms_norm(_dot(xb, wdkv_ref[...]), gkv_ref[...]).astype(BF16)
        kpe2 = _dot(xb, wkpe_ref[...])
        k_pe = kpe2[:, 0:LANES] * cos + kpe2[:, LANES:2 * LANES] * sin
        k_pe_even = jnp.where(first_half, k_pe, 0.0).astype(BF16)
        k_pe_odd = jnp.where(first_half, 0.0, k_pe).astype(BF16)
        k_nope = _dot(ckv, wuk_ref[...])
        for hd in range(MLA_HEADS):
            k_ref[_sub(t), hd * HEAD_W:hd * HEAD_W + LANES] = (
                k_nope[:, hd * QK_NOPE:(hd + 1) * QK_NOPE].astype(BF16))
            k_ref[_sub(t), hd * HEAD_W + LANES:(hd + 1) * HEAD_W] = k_pe_even if hd % 2 == 0 else k_pe_odd
        vt = lax.dot_general(wuvt_ref[...], ckv, NT_DIMS, preferred_element_type=F32)
        vt_ref[0, :, _sub(t)] = vt.astype(BF16)


def _mla_proj(x, cos, sin, wdq, gq, wuq, wdkv, gkv, wkpe, wuk, wuvt, batch, seq):
    n = x.shape[0]
    sps = seq // STEP_ROWS
    hw = MLA_HEADS * HEAD_W
    hv = MLA_HEADS * V_HEAD
    pos_spec = pl.BlockSpec((STEP_ROWS, LANES), lambda i: (i % sps, 0))
    return pl.pallas_call(
        _mla_proj_kernel,
        grid=(n // STEP_ROWS,),
        in_specs=[_row_spec(D_MODEL), pos_spec, pos_spec,
                  _const_spec((D_MODEL, Q_LORA)), _const_spec((1, Q_LORA)),
                  _const_spec((Q_LORA, MLA_HEADS * (QK_NOPE + 2 * QK_ROPE))),
                  _const_spec((D_MODEL, KV_LORA)), _const_spec((1, KV_LORA)),
                  _const_spec((D_MODEL, 2 * LANES)),
                  _const_spec((KV_LORA, MLA_HEADS * QK_NOPE)), _const_spec((hv, KV_LORA))],
        out_specs=[_row_spec(hw), _row_spec(hw),
                   pl.BlockSpec((1, hv, STEP_ROWS), lambda i: (i // sps, 0, i % sps))],
        out_shape=[jax.ShapeDtypeStruct((n, hw), BF16), jax.ShapeDtypeStruct((n, hw), BF16),
                   jax.ShapeDtypeStruct((batch, hv, seq), BF16)],
        compiler_params=_params("parallel"),
        name="mla_proj",
    )(x, cos, sin, wdq, gq, wuq, wdkv, gkv, wkpe, wuk, wuvt)


def _mla_attn_kernel(q_ref, k_ref, vt_ref, *refs, seq, n_cast):
    cast_src, o_ref, cast_dst = refs[:n_cast], refs[n_cast], refs[n_cast + 1:]
    _cast_blocks(cast_src, cast_dst)
    key_chunk = lax.broadcasted_iota(jnp.int32, (ATT_TQ, ATT_TQ), 0) // CHUNK
    qry_chunk = lax.broadcasted_iota(jnp.int32, (ATT_TQ, ATT_TQ), 1) // CHUNK
    diag_allowed = key_chunk <= qry_chunk

    def scores(hd, qi):
        lo, hi = qi * ATT_TQ, (qi + 1) * ATT_TQ
        hcols = slice(hd * HEAD_W, (hd + 1) * HEAD_W)
        q = q_ref[0, lo:hi, hcols]
        s_diag = lax.dot_general(k_ref[0, lo:hi, hcols], q, NT_DIMS, preferred_element_type=F32)
        s_diag = jnp.where(diag_allowed, s_diag, -1e30)
        m = jnp.max(s_diag, axis=0, keepdims=True)
        s_off = None
        if qi > 0:
            s_off = lax.dot_general(k_ref[0, 0:lo, hcols], q, NT_DIMS, preferred_element_type=F32)
            m = jnp.maximum(m, jnp.max(s_off, axis=0, keepdims=True))
        return s_diag, s_off, m

    def finish(hd, qi, s_diag, s_off, m):
        lo, hi = qi * ATT_TQ, (qi + 1) * ATT_TQ
        vrows = slice(hd * V_HEAD, (hd + 1) * V_HEAD)
        p_diag = jnp.exp2(s_diag - m)
        l = jnp.sum(p_diag, axis=0, keepdims=True)
        acc = _dot(vt_ref[0, vrows, lo:hi], p_diag.astype(BF16))
        if qi > 0:
            p_off = jnp.exp2(s_off - m)
            l = l + jnp.sum(p_off, axis=0, keepdims=True)
            acc = acc + _dot(vt_ref[0, vrows, 0:lo], p_off.astype(BF16))
        o_ref[0, vrows, lo:hi] = (acc / l).astype(BF16)

    items = [(hd, qi) for qi in range(seq // ATT_TQ) for hd in range(ATT_HEADS)]
    pending = scores(*items[0])
    for n, item in enumerate(items):
        nxt = scores(*items[n + 1]) if n + 1 < len(items) else None
        finish(*item, *pending)
        pending = nxt


def _mla_attn(q, k, vt, batch, seq, cast=()):
    assert ATT_TQ % CHUNK == 0 and seq % ATT_TQ == 0 and MLA_HEADS % ATT_HEADS == 0
    head_groups = MLA_HEADS // ATT_HEADS
    qk_spec = pl.BlockSpec((1, seq, ATT_HEADS * HEAD_W), lambda b, h: (b, 0, h))
    vt_spec = pl.BlockSpec((1, ATT_HEADS * V_HEAD, seq), lambda b, h: (b, h, 0))
    cast_in, cast_out, cast_shapes = _cast_specs(cast, batch * head_groups, lambda b, h: b * head_groups + h)
    return pl.pallas_call(
        functools.partial(_mla_attn_kernel, seq=seq, n_cast=len(cast)),
        grid=(batch, head_groups),
        in_specs=[qk_spec, qk_spec, vt_spec] + cast_in,
        out_specs=[vt_spec] + cast_out,
        out_shape=[jax.ShapeDtypeStruct((batch, MLA_HEADS * V_HEAD, seq), BF16)] + cast_shapes,
        compiler_params=_params("parallel", "parallel"),
        name="mla_attn",
    )(q.reshape(batch, seq, MLA_HEADS * HEAD_W), k.reshape(batch, seq, MLA_HEADS * HEAD_W), vt,
      *[w for w, _ in cast])


def _rope_tables(seq):
    pos = jnp.arange(seq, dtype=F32)
    inv_freq = ROPE_THETA ** (-jnp.arange(0, QK_ROPE, 2, dtype=F32) / QK_ROPE)
    ang = pos[:, None] * inv_freq[None, :]
    reps = LANES // (QK_ROPE // 2)
    return jnp.tile(jnp.cos(ang), (1, reps)), jnp.tile(jnp.sin(ang), (1, reps))


def _rotate_half_cols(w):
    half = w.shape[-1] // 2
    return jnp.concatenate([-w[..., half:], w[..., :half]], axis=-1)


def _mla_attention(x, w_dq, g_q, w_uq, w_dkv, g_kv, w_uk, w_uv, batch, seq, cast=()):
    wq = w_uq.reshape(Q_LORA, MLA_HEADS, QK_NOPE + QK_ROPE)
    wq_pe = wq[..., QK_NOPE:]
    wuq = jnp.concatenate(
        [wq[..., :QK_NOPE].reshape(Q_LORA, -1), wq_pe.reshape(Q_LORA, -1),
         _rotate_half_cols(wq_pe).reshape(Q_LORA, -1)], axis=1).astype(BF16)
    wk_pe = w_dkv[:, KV_LORA:]
    wk_pe_rot = _rotate_half_cols(wk_pe)
    wkpe = jnp.concatenate([wk_pe, wk_pe, wk_pe_rot, wk_pe_rot], axis=1).astype(BF16)
    wuvt = w_uv.reshape(KV_LORA, MLA_HEADS * V_HEAD).T.astype(BF16)
    cos, sin = _rope_tables(seq)
    q, k, vt = _mla_proj(x, cos, sin, w_dq.astype(BF16), g_q[None], wuq, w_dkv[:, :KV_LORA].astype(BF16),
                         g_kv[None], wkpe, w_uk.reshape(KV_LORA, -1).astype(BF16), wuvt, batch, seq)
    return _mla_attn(q, k, vt, batch, seq, cast=cast)


def kernel(x, sc_w_in, sc_conv_w, sc_w_out, mla_w_dq, mla_g_q, mla_w_uq, mla_w_dkv, mla_g_kv, mla_w_uk, mla_w_uv, mla_w_o, cf_w_pw1, cf_b_pw1, cf_dw_w, cf_dw_b, cf_norm_g, cf_norm_b, cf_w_pw2, cf_b_pw2, ff_w1, ff_w2, ln_mix_g, ln_mix_b, ln_ff_g, ln_ff_b):
    batch, seq, d = x.shape
    assert d == D_MODEL and seq % STEP_ROWS == 0
    h = x.reshape(batch * seq, d)

    calls = []
    for i in range(DEPTH):
        m, j = i % N_MIXERS, i // N_MIXERS
        mlp_w = [(ff_w1, i), (ff_w2, i)]
        if m == 1:
            calls.append(("attention", i, [], True))
            calls.append(("attn_out+mlp", i, mlp_w, False))
        elif m == 0 and i > 0:
            calls.append(("sconv+mlp", i, [(sc_w_in, j), (sc_w_out, j)] + mlp_w, False))
        else:
            mixer_w = [(sc_w_in, j), (sc_w_out, j)] if m == 0 else [(cf_w_pw1, j), (cf_w_pw2, j)]
            calls.append(("sconv" if m == 0 else "conf", i, mixer_w, True))
            calls.append(("mlp", i, mlp_w, True))
    jobs = [[] for _ in calls]
    source = [None] * len(calls)
    carrier = None
    for c, (_, _, needed, can_carry) in enumerate(calls):
        if needed and carrier is not None:
            source[c] = (carrier, len(jobs[carrier]))
            jobs[carrier].extend(needed)
        if can_carry:
            carrier = c
    casted = [None] * len(calls)

    def bf16_weights(c):
        needed = calls[c][2]
        if source[c] is None:
            return [w[layer].astype(BF16) for w, layer in needed]
        carrier, first = source[c]
        return casted[carrier][first:first + len(needed)]

    attn = None
    for c, (kind, i, _, _) in enumerate(calls):
        j = i // N_MIXERS
        g, b = ln_mix_g[i][None], ln_mix_b[i][None]
        weights = bf16_weights(c)
        mixer = {}
        if kind.startswith("sconv"):
            mixer = _sconv_mixer(weights[0], sc_conv_w[j], weights[1], g, b, seq)
        elif kind.startswith("conf"):
            mixer = _conf_mixer(weights[0], cf_b_pw1[j][None], cf_dw_w[j], cf_dw_b[j][None], cf_norm_g[j][None],
                                cf_norm_b[j][None], weights[1], cf_b_pw2[j][None], g, b, seq)
        elif kind.startswith("attn_out"):
            mixer = _attn_out_mixer(attn, mla_w_o[j].astype(BF16), g, b, seq)
        if kind == "attention":
            attn, *casted[c] = _mla_attention(h, mla_w_dq[j], mla_g_q[j], mla_w_uq[j], mla_w_dkv[j], mla_g_kv[j],
                                              mla_w_uk[j], mla_w_uv[j], batch, seq, cast=jobs[c])
            continue
        mlp = (*weights[-2:], ln_ff_g[i][None], ln_ff_b[i][None]) if kind.endswith("mlp") else None
        h, *casted[c] = _sublayer_call(kind.replace("+", "_") + "_ln", h, mlp=mlp, cast=jobs[c], **mixer)
    return h.reshape(batch, seq, d)
```

```python
import functools

import jax
import jax.numpy as jnp
from jax import lax
from jax.experimental import pallas as pl
from jax.experimental.pallas import tpu as pltpu

D_MODEL = 1024
DEPTH = 4
CHUNK = 64
N_MIXERS = 3
ALPHA = (2.0 * DEPTH) ** 0.25
LN_EPS = 1e-5
RMS_EPS = 1e-6
SC_WIDTH = 3
MLA_HEADS = 8
QK_NOPE = 128
QK_ROPE = 64
V_HEAD = 128
Q_LORA = 3 * D_MODEL // 8
KV_LORA = D_MODEL // 4
ROPE_THETA = 10000.0
CONF_WIDTH = 31
D_FF = 4 * D_MODEL

LANES = 128
SUBLANES = 8
N_SLABS = D_MODEL // LANES
HEAD_W = 2 * LANES
VMEM_LIMIT = 60 * 1024 * 1024

SUB_ROWS = 512
SUBTILES = 2
STEP_ROWS = SUBTILES * SUB_ROWS
ATT_TQ = 256
ATT_HEADS = 4
SC_HALO = 8
CF_HALO = 32
CONV_ROWS = 128
FF_CHUNKS = 4
TAIL_PIECES = 2
IN_CHUNK = 256
PROJ_CHUNK = 256

BF16 = jnp.bfloat16
F32 = jnp.float32
LOG2_E = 1.4426950408889634
NT_DIMS = (((1,), (1,)), ((), ()))
TN_DIMS = (((0,), (0,)), ((), ()))


def _dot(a, b):
    return jnp.dot(a, b, preferred_element_type=F32)


def _layer_norm(z, g, b):
    mu = jnp.mean(z, axis=-1, keepdims=True)
    zc = z - mu
    var = jnp.mean(zc * zc, axis=-1, keepdims=True)
    return zc * lax.rsqrt(var + LN_EPS) * g + b


def _rms_norm(z, g):
    return z * lax.rsqrt(jnp.mean(z * z, axis=-1, keepdims=True) + RMS_EPS) * g


def _sub(t):
    return slice(t * SUB_ROWS, (t + 1) * SUB_ROWS)


def _const_spec(shape):
    nd = len(shape)
    return pl.BlockSpec(shape, lambda *_: (0,) * nd, pipeline_mode=pl.Buffered(1))


def _row_spec(width):
    return pl.BlockSpec((STEP_ROWS, width), lambda i: (i, 0))


def _params(*sem):
    return pltpu.CompilerParams(dimension_semantics=sem, vmem_limit_bytes=VMEM_LIMIT)


def _cast_specs(jobs, n_steps, step=lambda i: i):
    in_specs, out_specs, shapes = [], [], []
    for w, layer in jobs:
        _, rows, cols = w.shape
        assert rows % n_steps == 0 and (rows // n_steps) % (2 * SUBLANES) == 0
        in_specs.append(pl.BlockSpec((None, rows // n_steps, cols),
                                     lambda *idx, layer=layer: (layer, step(*idx), 0)))
        out_specs.append(pl.BlockSpec((rows // n_steps, cols), lambda *idx: (step(*idx), 0)))
        shapes.append(jax.ShapeDtypeStruct((rows, cols), BF16))
    return in_specs, out_specs, shapes


def _cast_blocks(src_refs, dst_refs):
    for src, dst in zip(src_refs, dst_refs):
        dst[...] = src[...].astype(BF16)


def _mlp_stage(x_ref, w1_ref, w2_ref, g_ref, b_ref, o_ref):
    fc = D_FF // FF_CHUNKS

    def finish(rows, y):
        o_ref[rows, :] = _layer_norm(ALPHA * x_ref[rows, :] + y, g_ref[...], b_ref[...])

    pending = None
    for t in range(SUBTILES):
        xb = x_ref[_sub(t), :].astype(BF16)
        y = None
        for c in range(FF_CHUNKS):
            h = _dot(xb, w1_ref[:, c * fc:(c + 1) * fc])
            h = jnp.maximum(h, 0.0)
            h = (h * h).astype(BF16)
            if t == SUBTILES - 1 and c == FF_CHUNKS - 1:
                piece = SUB_ROWS // TAIL_PIECES
                for r in range(TAIL_PIECES):
                    local = slice(r * piece, (r + 1) * piece)
                    y_piece = y[local, :] + _dot(h[local, :], w2_ref[c * fc:(c + 1) * fc, :])
                    finish(slice(t * SUB_ROWS + r * piece, t * SUB_ROWS + (r + 1) * piece), y_piece)
                break
            yc = _dot(h, w2_ref[c * fc:(c + 1) * fc, :])
            y = yc if y is None else y + yc
            if c == 0 and pending is not None:
                finish(*pending)
        pending = (_sub(t), y)


def _sconv_stage(x_ref, win_ref, cw_ref, wout_ref, g_ref, b_ref, o_ref, ubuf, gbuf, *, steps_per_seq):
    @pl.when(pl.program_id(0) % steps_per_seq == 0)
    def _():
        ubuf[:, 0:SC_HALO, :] = jnp.zeros((N_SLABS, SC_HALO, LANES), F32)

    slabs_per_chunk = IN_CHUNK // LANES
    b_gates = []
    for t in range(SUBTILES):
        xb = x_ref[_sub(t), :].astype(BF16)
        slabs = []
        for c in range(D_MODEL // IN_CHUNK):
            b_cols, c_cols, h_cols = (slice(part * D_MODEL + c * IN_CHUNK, part * D_MODEL + (c + 1) * IN_CHUNK)
                                      for part in range(3))
            b_gate = _dot(xb, win_ref[:, b_cols])
            u = _dot(xb, win_ref[:, c_cols]) * _dot(xb, win_ref[:, h_cols])
            for i in range(slabs_per_chunk):
                lanes = slice(i * LANES, (i + 1) * LANES)
                slabs.append(b_gate[:, lanes])
                ubuf[c * slabs_per_chunk + i, SC_HALO + t * SUB_ROWS:SC_HALO + (t + 1) * SUB_ROWS, :] = u[:, lanes]
        b_gates.append(slabs)

    ys = []
    for t in range(SUBTILES):
        for s in range(N_SLABS):
            cols = slice(s * LANES, (s + 1) * LANES)
            conv = None
            for k in range(SC_WIDTH):
                start = SC_HALO + t * SUB_ROWS - (SC_WIDTH - 1) + k
                term = cw_ref[k:k + 1, cols] * ubuf[s, start:start + SUB_ROWS, :]
                conv = term if conv is None else conv + term
            gbuf[_sub(t), cols] = (b_gates[t][s] * conv).astype(BF16)
        ys.append(_dot(gbuf[_sub(t), :], wout_ref[...]))

    for t in range(SUBTILES):
        o_ref[_sub(t), :] = _layer_norm(ALPHA * x_ref[_sub(t), :] + ys[t], g_ref[...], b_ref[...])

    ubuf[:, 0:SC_HALO, :] = ubuf[:, STEP_ROWS:STEP_ROWS + SC_HALO, :]


def _conf_stage(x_ref, w1_ref, b1_ref, dww_ref, dwb_ref, ng_ref, nb_ref, w2_ref, b2_ref, g_ref, b_ref,
                o_ref, hbuf, cbuf, *, steps_per_seq):
    @pl.when(pl.program_id(0) % steps_per_seq == 0)
    def _():
        hbuf[:, 0:CF_HALO, :] = jnp.zeros((N_SLABS, CF_HALO, LANES), F32)

    slabs_per_chunk = IN_CHUNK // LANES
    for t in range(SUBTILES):
        xb = x_ref[_sub(t), :].astype(BF16)
        for c in range(D_MODEL // IN_CHUNK):
            a_cols = slice(c * IN_CHUNK, (c + 1) * IN_CHUNK)
            g_cols = slice(D_MODEL + c * IN_CHUNK, D_MODEL + (c + 1) * IN_CHUNK)
            a = _dot(xb, w1_ref[:, a_cols]) + b1_ref[:, a_cols]
            gate = _dot(xb, w1_ref[:, g_cols]) + b1_ref[:, g_cols]
            h = a * jax.nn.sigmoid(gate)
            for i in range(slabs_per_chunk):
                hbuf[c * slabs_per_chunk + i, CF_HALO + t * SUB_ROWS:CF_HALO + (t + 1) * SUB_ROWS, :] = (
                    h[:, i * LANES:(i + 1) * LANES])

    for t in range(SUBTILES):
        for s in range(N_SLABS):
            cols = slice(s * LANES, (s + 1) * LANES)
            for r0 in range(t * SUB_ROWS, (t + 1) * SUB_ROWS, CONV_ROWS):
                acc = jnp.zeros((CONV_ROWS, LANES), F32) + dwb_ref[:, cols]
                for k in range(CONF_WIDTH):
                    start = CF_HALO + r0 - (CONF_WIDTH - 1) + k
                    acc = acc + dww_ref[k:k + 1, cols] * hbuf[s, start:start + CONV_ROWS, :]
                cbuf[r0:r0 + CONV_ROWS, cols] = acc
        hn = _layer_norm(cbuf[_sub(t), :], ng_ref[...], nb_ref[...])
        hs = (hn * jax.nn.sigmoid(hn)).astype(BF16)
        y = _dot(hs, w2_ref[...]) + b2_ref[...]
        o_ref[_sub(t), :] = _layer_norm(ALPHA * x_ref[_sub(t), :] + y, g_ref[...], b_ref[...])

    hbuf[:, 0:CF_HALO, :] = hbuf[:, STEP_ROWS:STEP_ROWS + CF_HALO, :]


def _attn_out_stage(x_ref, ot_ref, wo_ref, g_ref, b_ref, o_ref):
    ys = [lax.dot_general(ot_ref[0, :, _sub(t)], wo_ref[...], TN_DIMS, preferred_element_type=F32)
          for t in range(SUBTILES)]
    for t in range(SUBTILES):
        o_ref[_sub(t), :] = _layer_norm(ALPHA * x_ref[_sub(t), :] + ys[t], g_ref[...], b_ref[...])


def _sublayer_kernel(x_ref, *refs, mixer, n_mixer_in, with_mlp, n_cast):
    mixer_in, refs = refs[:n_mixer_in], refs[n_mixer_in:]
    n_mlp_in = 4 if with_mlp else 0
    mlp_in, refs = refs[:n_mlp_in], refs[n_mlp_in:]
    cast_src, o_ref, cast_dst, scratch = refs[:n_cast], refs[n_cast], refs[n_cast + 1:2 * n_cast + 1], refs[2 * n_cast + 1:]
    _cast_blocks(cast_src, cast_dst)
    if mixer is not None:
        mixer(x_ref, *mixer_in, o_ref, *scratch)
    if with_mlp:
        _mlp_stage(x_ref if mixer is None else o_ref, *mlp_in, o_ref)


def _sublayer_call(name, x, mixer=None, mixer_args=(), mixer_specs=(), scratch=(), mlp=None, cast=()):
    n = x.shape[0]
    n_steps = n // STEP_ROWS
    vec = _const_spec((1, D_MODEL))
    mlp_specs = [] if mlp is None else [_const_spec((D_MODEL, D_FF)), _const_spec((D_FF, D_MODEL)), vec, vec]
    cast_in, cast_out, cast_shapes = _cast_specs(cast, n_steps)
    return pl.pallas_call(
        functools.partial(_sublayer_kernel, mixer=mixer, n_mixer_in=len(mixer_args),
                          with_mlp=mlp is not None, n_cast=len(cast)),
        grid=(n_steps,),
        in_specs=[_row_spec(D_MODEL)] + list(mixer_specs) + mlp_specs + cast_in,
        out_specs=[_row_spec(D_MODEL)] + cast_out,
        out_shape=[jax.ShapeDtypeStruct((n, D_MODEL), F32)] + cast_shapes,
        scratch_shapes=list(scratch),
        compiler_params=_params("parallel" if mixer is None else "arbitrary"),
        name=name,
    )(x, *mixer_args, *(mlp or ()), *[w for w, _ in cast])


def _sconv_mixer(w_in, conv_w, w_out, g, b, seq):
    vec = _const_spec((1, D_MODEL))
    return dict(
        mixer=functools.partial(_sconv_stage, steps_per_seq=seq // STEP_ROWS),
        mixer_args=(w_in, conv_w, w_out, g, b),
        mixer_specs=(_const_spec((D_MODEL, 3 * D_MODEL)), _const_spec((SC_WIDTH, D_MODEL)),
                     _const_spec((D_MODEL, D_MODEL)), vec, vec),
        scratch=(pltpu.VMEM((N_SLABS, SC_HALO + STEP_ROWS, LANES), F32), pltpu.VMEM((STEP_ROWS, D_MODEL), BF16)))


def _conf_mixer(w_pw1, b_pw1, dw_w, dw_b, norm_g, norm_b, w_pw2, b_pw2, g, b, seq):
    vec = _const_spec((1, D_MODEL))
    return dict(
        mixer=functools.partial(_conf_stage, steps_per_seq=seq // STEP_ROWS),
        mixer_args=(w_pw1, b_pw1, dw_w, dw_b, norm_g, norm_b, w_pw2, b_pw2, g, b),
        mixer_specs=(_const_spec((D_MODEL, 2 * D_MODEL)), _const_spec((1, 2 * D_MODEL)),
                     _const_spec((CONF_WIDTH, D_MODEL)), vec, vec, vec, _const_spec((D_MODEL, D_MODEL)), vec, vec, vec),
        scratch=(pltpu.VMEM((N_SLABS, CF_HALO + STEP_ROWS, LANES), F32), pltpu.VMEM((STEP_ROWS, D_MODEL), F32)))


def _attn_out_mixer(ot, w_o, g, b, seq):
    sps = seq // STEP_ROWS
    hv = MLA_HEADS * V_HEAD
    vec = _const_spec((1, D_MODEL))
    return dict(
        mixer=_attn_out_stage,
        mixer_args=(ot, w_o, g, b),
        mixer_specs=(pl.BlockSpec((1, hv, STEP_ROWS), lambda i: (i // sps, 0, i % sps)),
                     _const_spec((hv, D_MODEL)), vec, vec))


def _mla_proj_kernel(x_ref, cos_ref, sin_ref, wdq_ref, gq_ref, wuq_ref, wdkv_ref, gkv_ref, wkpe_ref,
                     wuk_ref, wuvt_ref, q_ref, k_ref, vt_ref):
    nope_w = MLA_HEADS * QK_NOPE
    rope_w = MLA_HEADS * QK_ROPE
    scale = (QK_NOPE + QK_ROPE) ** -0.5 * LOG2_E
    first_half = lax.broadcasted_iota(jnp.int32, (SUB_ROWS, LANES), 1) < QK_ROPE
    for t in range(SUBTILES):
        xb = x_ref[_sub(t), :].astype(BF16)
        cos = cos_ref[_sub(t), :]
        sin = sin_ref[_sub(t), :]

        cq = _rms_norm(_dot(xb, wdq_ref[...]), gq_ref[...]).astype(BF16)
        heads_per_chunk = PROJ_CHUNK // QK_NOPE
        for c in range(nope_w // PROJ_CHUNK):
            q_nope = _dot(cq, wuq_ref[:, c * PROJ_CHUNK:(c + 1) * PROJ_CHUNK]) * scale
            for i in range(heads_per_chunk):
                hd = c * heads_per_chunk + i
                q_ref[_sub(t), hd * HEAD_W:hd * HEAD_W + LANES] = q_nope[:, i * LANES:(i + 1) * LANES].astype(BF16)
        pairs_per_chunk = PROJ_CHUNK // LANES
        for c in range(rope_w // PROJ_CHUNK):
            pe = _dot(cq, wuq_ref[:, nope_w + c * PROJ_CHUNK:nope_w + (c + 1) * PROJ_CHUNK])
            pe_rot = _dot(cq, wuq_ref[:, nope_w + rope_w + c * PROJ_CHUNK:nope_w + rope_w + (c + 1) * PROJ_CHUNK])
            for i in range(pairs_per_chunk):
                lanes = slice(i * LANES, (i + 1) * LANES)
                slab = ((pe[:, lanes] * cos + pe_rot[:, lanes] * sin) * scale).astype(BF16)
                pair = c * pairs_per_chunk + i
                for hd in (2 * pair, 2 * pair + 1):
                    q_ref[_sub(t), hd * HEAD_W + LANES:(hd + 1) * HEAD_W] = slab

        ckv = _rms_norm(_dot(xb, wdkv_ref[...]), gkv_ref[...]).astype(BF16)
        kpe2 = _dot(xb, wkpe_ref[...])
        k_pe = kpe2[:, 0:LANES] * cos + kpe2[:, LANES:2 * LANES] * sin
        k_pe_even = jnp.where(first_half, k_pe, 0.0).astype(BF16)
        k_pe_odd = jnp.where(first_half, 0.0, k_pe).astype(BF16)
        for c in range(nope_w // PROJ_CHUNK):
            k_nope = _dot(ckv, wuk_ref[:, c * PROJ_CHUNK:(c + 1) * PROJ_CHUNK])
            for i in range(heads_per_chunk):
                hd = c * heads_per_chunk + i
                k_ref[_sub(t), hd * HEAD_W:hd * HEAD_W + LANES] = k_nope[:, i * LANES:(i + 1) * LANES].astype(BF16)
                k_ref[_sub(t), hd * HEAD_W + LANES:(hd + 1) * HEAD_W] = k_pe_even if hd % 2 == 0 else k_pe_odd
        for c in range(MLA_HEADS * V_HEAD // PROJ_CHUNK):
            rows = slice(c * PROJ_CHUNK, (c + 1) * PROJ_CHUNK)
            vt = lax.dot_general(wuvt_ref[rows, :], ckv, NT_DIMS, preferred_element_type=F32)
            vt_ref[0, rows, _sub(t)] = vt.astype(BF16)


def _mla_proj(x, cos, sin, wdq, gq, wuq, wdkv, gkv, wkpe, wuk, wuvt, batch, seq):
    n = x.shape[0]
    sps = seq // STEP_ROWS
    hw = MLA_HEADS * HEAD_W
    hv = MLA_HEADS * V_HEAD
    pos_spec = pl.BlockSpec((STEP_ROWS, LANES), lambda i: (i % sps, 0))
    return pl.pallas_call(
        _mla_proj_kernel,
        grid=(n // STEP_ROWS,),
        in_specs=[_row_spec(D_MODEL), pos_spec, pos_spec,
                  _const_spec((D_MODEL, Q_LORA)), _const_spec((1, Q_LORA)),
                  _const_spec((Q_LORA, MLA_HEADS * (QK_NOPE + 2 * QK_ROPE))),
                  _const_spec((D_MODEL, KV_LORA)), _const_spec((1, KV_LORA)),
                  _const_spec((D_MODEL, 2 * LANES)),
                  _const_spec((KV_LORA, MLA_HEADS * QK_NOPE)), _const_spec((hv, KV_LORA))],
        out_specs=[_row_spec(hw), _row_spec(hw),
                   pl.BlockSpec((1, hv, STEP_ROWS), lambda i: (i // sps, 0, i % sps))],
        out_shape=[jax.ShapeDtypeStruct((n, hw), BF16), jax.ShapeDtypeStruct((n, hw), BF16),
                   jax.ShapeDtypeStruct((batch, hv, seq), BF16)],
        compiler_params=_params("parallel"),
        name="mla_proj",
    )(x, cos, sin, wdq, gq, wuq, wdkv, gkv, wkpe, wuk, wuvt)


def _mla_attn_kernel(q_ref, k_ref, vt_ref, *refs, seq, n_cast):
    cast_src, o_ref, cast_dst = refs[:n_cast], refs[n_cast], refs[n_cast + 1:]
    _cast_blocks(cast_src, cast_dst)
    key_chunk = lax.broadcasted_iota(jnp.int32, (ATT_TQ, ATT_TQ), 0) // CHUNK
    qry_chunk = lax.broadcasted_iota(jnp.int32, (ATT_TQ, ATT_TQ), 1) // CHUNK
    diag_allowed = key_chunk <= qry_chunk

    def scores(hd, qi):
        lo, hi = qi * ATT_TQ, (qi + 1) * ATT_TQ
        hcols = slice(hd * HEAD_W, (hd + 1) * HEAD_W)
        q = q_ref[0, lo:hi, hcols]
        s_diag = lax.dot_general(k_ref[0, lo:hi, hcols], q, NT_DIMS, preferred_element_type=F32)
        s_diag = jnp.where(diag_allowed, s_diag, -1e30)
        m = jnp.max(s_diag, axis=0, keepdims=True)
        s_off = None
        if qi > 0:
            s_off = lax.dot_general(k_ref[0, 0:lo, hcols], q, NT_DIMS, preferred_element_type=F32)
            m = jnp.maximum(m, jnp.max(s_off, axis=0, keepdims=True))
        return s_diag, s_off, m

    def finish(hd, qi, s_diag, s_off, m):
        lo, hi = qi * ATT_TQ, (qi + 1) * ATT_TQ
        vrows = slice(hd * V_HEAD, (hd + 1) * V_HEAD)
        p_diag = jnp.exp2(s_diag - m)
        l = jnp.sum(p_diag, axis=0, keepdims=True)
        acc = _dot(vt_ref[0, vrows, lo:hi], p_diag.astype(BF16))
        if qi > 0:
            p_off = jnp.exp2(s_off - m)
            l = l + jnp.sum(p_off, axis=0, keepdims=True)
            acc = acc + _dot(vt_ref[0, vrows, 0:lo], p_off.astype(BF16))
        o_ref[0, vrows, lo:hi] = (acc / l).astype(BF16)

    items = [(hd, qi) for qi in range(seq // ATT_TQ) for hd in range(ATT_HEADS)]
    pending = scores(*items[0])
    for n, item in enumerate(items):
        nxt = scores(*items[n + 1]) if n + 1 < len(items) else None
        finish(*item, *pending)
        pending = nxt


def _mla_attn(q, k, vt, batch, seq, cast=()):
    assert ATT_TQ % CHUNK == 0 and seq % ATT_TQ == 0 and MLA_HEADS % ATT_HEADS == 0
    head_groups = MLA_HEADS // ATT_HEADS
    qk_spec = pl.BlockSpec((1, seq, ATT_HEADS * HEAD_W), lambda b, h: (b, 0, h))
    vt_spec = pl.BlockSpec((1, ATT_HEADS * V_HEAD, seq), lambda b, h: (b, h, 0))
    cast_in, cast_out, cast_shapes = _cast_specs(cast, batch * head_groups, lambda b, h: b * head_groups + h)
    return pl.pallas_call(
        functools.partial(_mla_attn_kernel, seq=seq, n_cast=len(cast)),
        grid=(batch, head_groups),
        in_specs=[qk_spec, qk_spec, vt_spec] + cast_in,
        out_specs=[vt_spec] + cast_out,
        out_shape=[jax.ShapeDtypeStruct((batch, MLA_HEADS * V_HEAD, seq), BF16)] + cast_shapes,
        compiler_params=_params("parallel", "parallel"),
        name="mla_attn",
    )(q.reshape(batch, seq, MLA_HEADS * HEAD_W), k.reshape(batch, seq, MLA_HEADS * HEAD_W), vt,
      *[w for w, _ in cast])


def _rope_tables(seq):
    pos = jnp.arange(seq, dtype=F32)
    inv_freq = ROPE_THETA ** (-jnp.arange(0, QK_ROPE, 2, dtype=F32) / QK_ROPE)
    ang = pos[:, None] * inv_freq[None, :]
    reps = LANES // (QK_ROPE // 2)
    return jnp.tile(jnp.cos(ang), (1, reps)), jnp.tile(jnp.sin(ang), (1, reps))


def _rotate_half_cols(w):
    half = w.shape[-1] // 2
    return jnp.concatenate([-w[..., half:], w[..., :half]], axis=-1)


def _mla_attention(x, w_dq, g_q, w_uq, w_dkv, g_kv, w_uk, w_uv, batch, seq, cast=()):
    wq = w_uq.reshape(Q_LORA, MLA_HEADS, QK_NOPE + QK_ROPE)
    wq_pe = wq[..., QK_NOPE:]
    wuq = jnp.concatenate(
        [wq[..., :QK_NOPE].reshape(Q_LORA, -1), wq_pe.reshape(Q_LORA, -1),
         _rotate_half_cols(wq_pe).reshape(Q_LORA, -1)], axis=1).astype(BF16)
    wk_pe = w_dkv[:, KV_LORA:]
    wk_pe_rot = _rotate_half_cols(wk_pe)
    wkpe = jnp.concatenate([wk_pe, wk_pe, wk_pe_rot, wk_pe_rot], axis=1).astype(BF16)
    wuvt = w_uv.reshape(KV_LORA, MLA_HEADS * V_HEAD).T.astype(BF16)
    cos, sin = _rope_tables(seq)
    q, k, vt = _mla_proj(x, cos, sin, w_dq.astype(BF16), g_q[None], wuq, w_dkv[:, :KV_LORA].astype(BF16),
                         g_kv[None], wkpe, w_uk.reshape(KV_LORA, -1).astype(BF16), wuvt, batch, seq)
    return _mla_attn(q, k, vt, batch, seq, cast=cast)


def kernel(x, sc_w_in, sc_conv_w, sc_w_out, mla_w_dq, mla_g_q, mla_w_uq, mla_w_dkv, mla_g_kv, mla_w_uk, mla_w_uv, mla_w_o, cf_w_pw1, cf_b_pw1, cf_dw_w, cf_dw_b, cf_norm_g, cf_norm_b, cf_w_pw2, cf_b_pw2, ff_w1, ff_w2, ln_mix_g, ln_mix_b, ln_ff_g, ln_ff_b):
    batch, seq, d = x.shape
    assert d == D_MODEL and seq % STEP_ROWS == 0
    h = x.reshape(batch * seq, d)

    calls = []
    for i in range(DEPTH):
        m, j = i % N_MIXERS, i // N_MIXERS
        mlp_w = [(ff_w1, i), (ff_w2, i)]
        if m == 1:
            calls.append(("attention", i, [], True))
            calls.append(("attn_out+mlp", i, mlp_w, False))
        elif m == 0 and i > 0:
            calls.append(("sconv+mlp", i, [(sc_w_in, j), (sc_w_out, j)] + mlp_w, False))
        else:
            mixer_w = [(sc_w_in, j), (sc_w_out, j)] if m == 0 else [(cf_w_pw1, j), (cf_w_pw2, j)]
            calls.append(("sconv" if m == 0 else "conf", i, mixer_w, True))
            calls.append(("mlp", i, mlp_w, True))
    jobs = [[] for _ in calls]
    source = [None] * len(calls)
    carrier = None
    for c, (_, _, needed, can_carry) in enumerate(calls):
        if needed and carrier is not None:
            source[c] = (carrier, len(jobs[carrier]))
            jobs[carrier].extend(needed)
        if can_carry:
            carrier = c
    casted = [None] * len(calls)

    def bf16_weights(c):
        needed = calls[c][2]
        if source[c] is None:
            return [w[layer].astype(BF16) for w, layer in needed]
        carrier, first = source[c]
        return casted[carrier][first:first + len(needed)]

    attn = None
    for c, (kind, i, _, _) in enumerate(calls):
        j = i // N_MIXERS
        g, b = ln_mix_g[i][None], ln_mix_b[i][None]
        weights = bf16_weights(c)
        mixer = {}
        if kind.startswith("sconv"):
            mixer = _sconv_mixer(weights[0], sc_conv_w[j], weights[1], g, b, seq)
        elif kind.startswith("conf"):
            mixer = _conf_mixer(weights[0], cf_b_pw1[j][None], cf_dw_w[j], cf_dw_b[j][None], cf_norm_g[j][None],
                                cf_norm_b[j][None], weights[1], cf_b_pw2[j][None], g, b, seq)
        elif kind.startswith("attn_out"):
            mixer = _attn_out_mixer(attn, mla_w_o[j].astype(BF16), g, b, seq)
        if kind == "attention":
            attn, *casted[c] = _mla_attention(h, mla_w_dq[j], mla_g_q[j], mla_w_uq[j], mla_w_dkv[j], mla_g_kv[j],
                                              mla_w_uk[j], mla_w_uv[j], batch, seq, cast=jobs[c])
            continue
        mlp = (*weights[-2:], ln_ff_g[i][None], ln_ff_b[i][None]) if kind.endswith("mlp") else None
        h, *casted[c] = _sublayer_call(kind.replace("+", "_") + "_ln", h, mlp=mlp, cast=jobs[c], **mixer)
    return h.reshape(batch, seq, d)
```

```python
import functools

import jax
import jax.numpy as jnp
from jax import lax
from jax.experimental import pallas as pl
from jax.experimental.pallas import tpu as pltpu

D_MODEL = 1024
DEPTH = 4
CHUNK = 64
N_MIXERS = 3
ALPHA = (2.0 * DEPTH) ** 0.25
LN_EPS = 1e-5
RMS_EPS = 1e-6
SC_WIDTH = 3
MLA_HEADS = 8
QK_NOPE = 128
QK_ROPE = 64
V_HEAD = 128
Q_LORA = 3 * D_MODEL // 8
KV_LORA = D_MODEL // 4
ROPE_THETA = 10000.0
CONF_WIDTH = 31
D_FF = 4 * D_MODEL

LANES = 128
SUBLANES = 8
N_SLABS = D_MODEL // LANES
HEAD_W = 2 * LANES
VMEM_LIMIT = 60 * 1024 * 1024

SUB_ROWS = 512
SUBTILES = 2
STEP_ROWS = SUBTILES * SUB_ROWS
FINE_SUB_ROWS = 256
ATT_TQ = 256
ATT_HEADS = 4
SC_HALO = 8
CF_HALO = 32
CONV_ROWS = 128
FF_CHUNKS = 4
TAIL_PIECES = 2
IN_CHUNK = 256
PROJ_CHUNK = 256

BF16 = jnp.bfloat16
F32 = jnp.float32
LOG2_E = 1.4426950408889634
NT_DIMS = (((1,), (1,)), ((), ()))
TN_DIMS = (((0,), (0,)), ((), ()))


def _dot(a, b):
    return jnp.dot(a, b, preferred_element_type=F32)


def _layer_norm(z, g, b):
    mu = jnp.mean(z, axis=-1, keepdims=True)
    zc = z - mu
    var = jnp.mean(zc * zc, axis=-1, keepdims=True)
    return zc * lax.rsqrt(var + LN_EPS) * g + b


def _rms_norm(z, g):
    return z * lax.rsqrt(jnp.mean(z * z, axis=-1, keepdims=True) + RMS_EPS) * g


def _sub(t):
    return slice(t * SUB_ROWS, (t + 1) * SUB_ROWS)


def _const_spec(shape):
    nd = len(shape)
    return pl.BlockSpec(shape, lambda *_: (0,) * nd, pipeline_mode=pl.Buffered(1))


def _row_spec(width):
    return pl.BlockSpec((STEP_ROWS, width), lambda i: (i, 0))


def _params(*sem):
    return pltpu.CompilerParams(dimension_semantics=sem, vmem_limit_bytes=VMEM_LIMIT)


def _cast_specs(jobs, n_steps, step=lambda i: i):
    in_specs, out_specs, shapes = [], [], []
    for w, layer in jobs:
        _, rows, cols = w.shape
        assert rows % n_steps == 0 and (rows // n_steps) % (2 * SUBLANES) == 0
        in_specs.append(pl.BlockSpec((None, rows // n_steps, cols),
                                     lambda *idx, layer=layer: (layer, step(*idx), 0)))
        out_specs.append(pl.BlockSpec((rows // n_steps, cols), lambda *idx: (step(*idx), 0)))
        shapes.append(jax.ShapeDtypeStruct((rows, cols), BF16))
    return in_specs, out_specs, shapes


def _cast_blocks(src_refs, dst_refs):
    for src, dst in zip(src_refs, dst_refs):
        dst[...] = src[...].astype(BF16)


def _mlp_stage(x_ref, w1_ref, w2_ref, g_ref, b_ref, o_ref):
    fc = D_FF // FF_CHUNKS
    rows_per, n_sub = FINE_SUB_ROWS, STEP_ROWS // FINE_SUB_ROWS

    def sub(t):
        return slice(t * rows_per, (t + 1) * rows_per)

    def finish(rows, y):
        o_ref[rows, :] = _layer_norm(ALPHA * x_ref[rows, :] + y, g_ref[...], b_ref[...])

    pending = None
    for t in range(n_sub):
        xb = x_ref[sub(t), :].astype(BF16)
        y = None
        for c in range(FF_CHUNKS):
            h = _dot(xb, w1_ref[:, c * fc:(c + 1) * fc])
            h = jnp.maximum(h, 0.0)
            h = (h * h).astype(BF16)
            if t == n_sub - 1 and c == FF_CHUNKS - 1:
                piece = rows_per // TAIL_PIECES
                for r in range(TAIL_PIECES):
                    local = slice(r * piece, (r + 1) * piece)
                    y_piece = y[local, :] + _dot(h[local, :], w2_ref[c * fc:(c + 1) * fc, :])
                    finish(slice(t * rows_per + r * piece, t * rows_per + (r + 1) * piece), y_piece)
                break
            yc = _dot(h, w2_ref[c * fc:(c + 1) * fc, :])
            y = yc if y is None else y + yc
            if c == 0 and pending is not None:
                finish(*pending)
        pending = (sub(t), y)


def _sconv_stage(x_ref, win_ref, cw_ref, wout_ref, g_ref, b_ref, o_ref, ubuf, gbuf, *, steps_per_seq):
    rows_per, n_sub, sub = SUB_ROWS, SUBTILES, _sub

    @pl.when(pl.program_id(0) % steps_per_seq == 0)
    def _():
        ubuf[:, 0:SC_HALO, :] = jnp.zeros((N_SLABS, SC_HALO, LANES), F32)

    slabs_per_chunk = IN_CHUNK // LANES
    b_gates = []
    for t in range(n_sub):
        xb = x_ref[sub(t), :].astype(BF16)
        slabs = []
        for c in range(D_MODEL // IN_CHUNK):
            b_cols, c_cols, h_cols = (slice(part * D_MODEL + c * IN_CHUNK, part * D_MODEL + (c + 1) * IN_CHUNK)
                                      for part in range(3))
            b_gate = _dot(xb, win_ref[:, b_cols])
            u = _dot(xb, win_ref[:, c_cols]) * _dot(xb, win_ref[:, h_cols])
            for i in range(slabs_per_chunk):
                lanes = slice(i * LANES, (i + 1) * LANES)
                slabs.append(b_gate[:, lanes])
                ubuf[c * slabs_per_chunk + i, SC_HALO + t * rows_per:SC_HALO + (t + 1) * rows_per, :] = u[:, lanes]
        b_gates.append(slabs)

    ys = []
    for t in range(n_sub):
        for s in range(N_SLABS):
            cols = slice(s * LANES, (s + 1) * LANES)
            conv = None
            for k in range(SC_WIDTH):
                start = SC_HALO + t * rows_per - (SC_WIDTH - 1) + k
                term = cw_ref[k:k + 1, cols] * ubuf[s, start:start + rows_per, :]
                conv = term if conv is None else conv + term
            gbuf[sub(t), cols] = (b_gates[t][s] * conv).astype(BF16)
        ys.append(_dot(gbuf[sub(t), :], wout_ref[...]))

    for t in range(n_sub):
        o_ref[sub(t), :] = _layer_norm(ALPHA * x_ref[sub(t), :] + ys[t], g_ref[...], b_ref[...])

    ubuf[:, 0:SC_HALO, :] = ubuf[:, STEP_ROWS:STEP_ROWS + SC_HALO, :]


def _conf_stage(x_ref, w1_ref, b1_ref, dww_ref, dwb_ref, ng_ref, nb_ref, w2_ref, b2_ref, g_ref, b_ref,
                o_ref, hbuf, cbuf, *, steps_per_seq):
    @pl.when(pl.program_id(0) % steps_per_seq == 0)
    def _():
        hbuf[:, 0:CF_HALO, :] = jnp.zeros((N_SLABS, CF_HALO, LANES), F32)

    slabs_per_chunk = IN_CHUNK // LANES
    for t in range(SUBTILES):
        xb = x_ref[_sub(t), :].astype(BF16)
        for c in range(D_MODEL // IN_CHUNK):
            a_cols = slice(c * IN_CHUNK, (c + 1) * IN_CHUNK)
            g_cols = slice(D_MODEL + c * IN_CHUNK, D_MODEL + (c + 1) * IN_CHUNK)
            a = _dot(xb, w1_ref[:, a_cols]) + b1_ref[:, a_cols]
            gate = _dot(xb, w1_ref[:, g_cols]) + b1_ref[:, g_cols]
            h = a * jax.nn.sigmoid(gate)
            for i in range(slabs_per_chunk):
                hbuf[c * slabs_per_chunk + i, CF_HALO + t * SUB_ROWS:CF_HALO + (t + 1) * SUB_ROWS, :] = (
                    h[:, i * LANES:(i + 1) * LANES])

    for t in range(SUBTILES):
        for s in range(N_SLABS):
            cols = slice(s * LANES, (s + 1) * LANES)
            for r0 in range(t * SUB_ROWS, (t + 1) * SUB_ROWS, CONV_ROWS):
                acc = jnp.zeros((CONV_ROWS, LANES), F32) + dwb_ref[:, cols]
                for k in range(CONF_WIDTH):
                    start = CF_HALO + r0 - (CONF_WIDTH - 1) + k
                    acc = acc + dww_ref[k:k + 1, cols] * hbuf[s, start:start + CONV_ROWS, :]
                cbuf[r0:r0 + CONV_ROWS, cols] = acc
        hn = _layer_norm(cbuf[_sub(t), :], ng_ref[...], nb_ref[...])
        hs = (hn * jax.nn.sigmoid(hn)).astype(BF16)
        y = _dot(hs, w2_ref[...]) + b2_ref[...]
        o_ref[_sub(t), :] = _layer_norm(ALPHA * x_ref[_sub(t), :] + y, g_ref[...], b_ref[...])

    hbuf[:, 0:CF_HALO, :] = hbuf[:, STEP_ROWS:STEP_ROWS + CF_HALO, :]


def _attn_out_stage(x_ref, ot_ref, wo_ref, g_ref, b_ref, o_ref):
    ys = [lax.dot_general(ot_ref[0, :, _sub(t)], wo_ref[...], TN_DIMS, preferred_element_type=F32)
          for t in range(SUBTILES)]
    for t in range(SUBTILES):
        o_ref[_sub(t), :] = _layer_norm(ALPHA * x_ref[_sub(t), :] + ys[t], g_ref[...], b_ref[...])


def _sublayer_kernel(x_ref, *refs, mixer, n_mixer_in, with_mlp, n_cast):
    mixer_in, refs = refs[:n_mixer_in], refs[n_mixer_in:]
    n_mlp_in = 4 if with_mlp else 0
    mlp_in, refs = refs[:n_mlp_in], refs[n_mlp_in:]
    cast_src, o_ref, cast_dst, scratch = refs[:n_cast], refs[n_cast], refs[n_cast + 1:2 * n_cast + 1], refs[2 * n_cast + 1:]
    _cast_blocks(cast_src, cast_dst)
    if mixer is not None:
        mixer(x_ref, *mixer_in, o_ref, *scratch)
    if with_mlp:
        _mlp_stage(x_ref if mixer is None else o_ref, *mlp_in, o_ref)


def _sublayer_call(name, x, mixer=None, mixer_args=(), mixer_specs=(), scratch=(), mlp=None, cast=()):
    n = x.shape[0]
    n_steps = n // STEP_ROWS
    vec = _const_spec((1, D_MODEL))
    mlp_specs = [] if mlp is None else [_const_spec((D_MODEL, D_FF)), _const_spec((D_FF, D_MODEL)), vec, vec]
    cast_in, cast_out, cast_shapes = _cast_specs(cast, n_steps)
    return pl.pallas_call(
        functools.partial(_sublayer_kernel, mixer=mixer, n_mixer_in=len(mixer_args),
                          with_mlp=mlp is not None, n_cast=len(cast)),
        grid=(n_steps,),
        in_specs=[_row_spec(D_MODEL)] + list(mixer_specs) + mlp_specs + cast_in,
        out_specs=[_row_spec(D_MODEL)] + cast_out,
        out_shape=[jax.ShapeDtypeStruct((n, D_MODEL), F32)] + cast_shapes,
        scratch_shapes=list(scratch),
        compiler_params=_params("parallel" if mixer is None else "arbitrary"),
        name=name,
    )(x, *mixer_args, *(mlp or ()), *[w for w, _ in cast])


def _sconv_mixer(w_in, conv_w, w_out, g, b, seq):
    vec = _const_spec((1, D_MODEL))
    return dict(
        mixer=functools.partial(_sconv_stage, steps_per_seq=seq // STEP_ROWS),
        mixer_args=(w_in, conv_w, w_out, g, b),
        mixer_specs=(_const_spec((D_MODEL, 3 * D_MODEL)), _const_spec((SC_WIDTH, D_MODEL)),
                     _const_spec((D_MODEL, D_MODEL)), vec, vec),
        scratch=(pltpu.VMEM((N_SLABS, SC_HALO + STEP_ROWS, LANES), F32), pltpu.VMEM((STEP_ROWS, D_MODEL), BF16)))


def _conf_mixer(w_pw1, b_pw1, dw_w, dw_b, norm_g, norm_b, w_pw2, b_pw2, g, b, seq):
    vec = _const_spec((1, D_MODEL))
    return dict(
        mixer=functools.partial(_conf_stage, steps_per_seq=seq // STEP_ROWS),
        mixer_args=(w_pw1, b_pw1, dw_w, dw_b, norm_g, norm_b, w_pw2, b_pw2, g, b),
        mixer_specs=(_const_spec((D_MODEL, 2 * D_MODEL)), _const_spec((1, 2 * D_MODEL)),
                     _const_spec((CONF_WIDTH, D_MODEL)), vec, vec, vec, _const_spec((D_MODEL, D_MODEL)), vec, vec, vec),
        scratch=(pltpu.VMEM((N_SLABS, CF_HALO + STEP_ROWS, LANES), F32), pltpu.VMEM((STEP_ROWS, D_MODEL), F32)))


def _attn_out_mixer(ot, w_o, g, b, seq):
    sps = seq // STEP_ROWS
    hv = MLA_HEADS * V_HEAD
    vec = _const_spec((1, D_MODEL))
    return dict(
        mixer=_attn_out_stage,
        mixer_args=(ot, w_o, g, b),
        mixer_specs=(pl.BlockSpec((1, hv, STEP_ROWS), lambda i: (i // sps, 0, i % sps)),
                     _const_spec((hv, D_MODEL)), vec, vec))


def _mla_proj_kernel(x_ref, cos_ref, sin_ref, wdq_ref, gq_ref, wuq_ref, wdkv_ref, gkv_ref, wkpe_ref,
                     wuk_ref, wuvt_ref, q_ref, k_ref, vt_ref):
    nope_w = MLA_HEADS * QK_NOPE
    rope_w = MLA_HEADS * QK_ROPE
    scale = (QK_NOPE + QK_ROPE) ** -0.5 * LOG2_E
    first_half = lax.broadcasted_iota(jnp.int32, (SUB_ROWS, LANES), 1) < QK_ROPE
    for t in range(SUBTILES):
        xb = x_ref[_sub(t), :].astype(BF16)
        cos = cos_ref[_sub(t), :]
        sin = sin_ref[_sub(t), :]

        cq = _rms_norm(_dot(xb, wdq_ref[...]), gq_ref[...]).astype(BF16)
        heads_per_chunk = PROJ_CHUNK // QK_NOPE
        for c in range(nope_w // PROJ_CHUNK):
            q_nope = _dot(cq, wuq_ref[:, c * PROJ_CHUNK:(c + 1) * PROJ_CHUNK]) * scale
            for i in range(heads_per_chunk):
                hd = c * heads_per_chunk + i
                q_ref[_sub(t), hd * HEAD_W:hd * HEAD_W + LANES] = q_nope[:, i * LANES:(i + 1) * LANES].astype(BF16)
        pairs_per_chunk = PROJ_CHUNK // LANES
        for c in range(rope_w // PROJ_CHUNK):
            pe = _dot(cq, wuq_ref[:, nope_w + c * PROJ_CHUNK:nope_w + (c + 1) * PROJ_CHUNK])
            pe_rot = _dot(cq, wuq_ref[:, nope_w + rope_w + c * PROJ_CHUNK:nope_w + rope_w + (c + 1) * PROJ_CHUNK])
            for i in range(pairs_per_chunk):
                lanes = slice(i * LANES, (i + 1) * LANES)
                slab = ((pe[:, lanes] * cos + pe_rot[:, lanes] * sin) * scale).astype(BF16)
                pair = c * pairs_per_chunk + i
                for hd in (2 * pair, 2 * pair + 1):
                    q_ref[_sub(t), hd * HEAD_W + LANES:(hd + 1) * HEAD_W] = slab

        ckv = _rms_norm(_dot(xb, wdkv_ref[...]), gkv_ref[...]).astype(BF16)
        kpe2 = _dot(xb, wkpe_ref[...])
        k_pe = kpe2[:, 0:LANES] * cos + kpe2[:, LANES:2 * LANES] * sin
        k_pe_even = jnp.where(first_half, k_pe, 0.0).astype(BF16)
        k_pe_odd = jnp.where(first_half, 0.0, k_pe).astype(BF16)
        for c in range(nope_w // PROJ_CHUNK):
            k_nope = _dot(ckv, wuk_ref[:, c * PROJ_CHUNK:(c + 1) * PROJ_CHUNK])
            for i in range(heads_per_chunk):
                hd = c * heads_per_chunk + i
                k_ref[_sub(t), hd * HEAD_W:hd * HEAD_W + LANES] = k_nope[:, i * LANES:(i + 1) * LANES].astype(BF16)
                k_ref[_sub(t), hd * HEAD_W + LANES:(hd + 1) * HEAD_W] = k_pe_even if hd % 2 == 0 else k_pe_odd
        for c in range(MLA_HEADS * V_HEAD // PROJ_CHUNK):
            rows = slice(c * PROJ_CHUNK, (c + 1) * PROJ_CHUNK)
            vt = lax.dot_general(wuvt_ref[rows, :], ckv, NT_DIMS, preferred_element_type=F32)
            vt_ref[0, rows, _sub(t)] = vt.astype(BF16)


def _mla_proj(x, cos, sin, wdq, gq, wuq, wdkv, gkv, wkpe, wuk, wuvt, batch, seq):
    n = x.shape[0]
    sps = seq // STEP_ROWS
    hw = MLA_HEADS * HEAD_W
    hv = MLA_HEADS * V_HEAD
    pos_spec = pl.BlockSpec((STEP_ROWS, LANES), lambda i: (i % sps, 0))
    return pl.pallas_call(
        _mla_proj_kernel,
        grid=(n // STEP_ROWS,),
        in_specs=[_row_spec(D_MODEL), pos_spec, pos_spec,
                  _const_spec((D_MODEL, Q_LORA)), _const_spec((1, Q_LORA)),
                  _const_spec((Q_LORA, MLA_HEADS * (QK_NOPE + 2 * QK_ROPE))),
                  _const_spec((D_MODEL, KV_LORA)), _const_spec((1, KV_LORA)),
                  _const_spec((D_MODEL, 2 * LANES)),
                  _const_spec((KV_LORA, MLA_HEADS * QK_NOPE)), _const_spec((hv, KV_LORA))],
        out_specs=[_row_spec(hw), _row_spec(hw),
                   pl.BlockSpec((1, hv, STEP_ROWS), lambda i: (i // sps, 0, i % sps))],
        out_shape=[jax.ShapeDtypeStruct((n, hw), BF16), jax.ShapeDtypeStruct((n, hw), BF16),
                   jax.ShapeDtypeStruct((batch, hv, seq), BF16)],
        compiler_params=_params("parallel"),
        name="mla_proj",
    )(x, cos, sin, wdq, gq, wuq, wdkv, gkv, wkpe, wuk, wuvt)


def _mla_attn_kernel(q_ref, k_ref, vt_ref, *refs, seq, n_cast):
    cast_src, o_ref, cast_dst = refs[:n_cast], refs[n_cast], refs[n_cast + 1:]
    _cast_blocks(cast_src, cast_dst)
    key_chunk = lax.broadcasted_iota(jnp.int32, (ATT_TQ, ATT_TQ), 0) // CHUNK
    qry_chunk = lax.broadcasted_iota(jnp.int32, (ATT_TQ, ATT_TQ), 1) // CHUNK
    diag_allowed = key_chunk <= qry_chunk

    def scores(hd, qi):
        lo, hi = qi * ATT_TQ, (qi + 1) * ATT_TQ
        hcols = slice(hd * HEAD_W, (hd + 1) * HEAD_W)
        q = q_ref[0, lo:hi, hcols]
        s_diag = lax.dot_general(k_ref[0, lo:hi, hcols], q, NT_DIMS, preferred_element_type=F32)
        s_diag = jnp.where(diag_allowed, s_diag, -1e30)
        m = jnp.max(s_diag, axis=0, keepdims=True)
        s_off = None
        if qi > 0:
            s_off = lax.dot_general(k_ref[0, 0:lo, hcols], q, NT_DIMS, preferred_element_type=F32)
            m = jnp.maximum(m, jnp.max(s_off, axis=0, keepdims=True))
        return s_diag, s_off, m

    def finish(hd, qi, s_diag, s_off, m):
        lo, hi = qi * ATT_TQ, (qi + 1) * ATT_TQ
        vrows = slice(hd * V_HEAD, (hd + 1) * V_HEAD)
        p_diag = jnp.exp2(s_diag - m)
        l = jnp.sum(p_diag, axis=0, keepdims=True)
        acc = _dot(vt_ref[0, vrows, lo:hi], p_diag.astype(BF16))
        if qi > 0:
            p_off = jnp.exp2(s_off - m)
            l = l + jnp.sum(p_off, axis=0, keepdims=True)
            acc = acc + _dot(vt_ref[0, vrows, 0:lo], p_off.astype(BF16))
        o_ref[0, vrows, lo:hi] = (acc / l).astype(BF16)

    items = [(hd, qi) for qi in range(seq // ATT_TQ) for hd in range(ATT_HEADS)]
    pending = scores(*items[0])
    for n, item in enumerate(items):
        nxt = scores(*items[n + 1]) if n + 1 < len(items) else None
        finish(*item, *pending)
        pending = nxt


def _mla_attn(q, k, vt, batch, seq, cast=()):
    assert ATT_TQ % CHUNK == 0 and seq % ATT_TQ == 0 and MLA_HEADS % ATT_HEADS == 0
    head_groups = MLA_HEADS // ATT_HEADS
    qk_spec = pl.BlockSpec((1, seq, ATT_HEADS * HEAD_W), lambda b, h: (b, 0, h))
    vt_spec = pl.BlockSpec((1, ATT_HEADS * V_HEAD, seq), lambda b, h: (b, h, 0))
    cast_in, cast_out, cast_shapes = _cast_specs(cast, batch * head_groups, lambda b, h: b * head_groups + h)
    return pl.pallas_call(
        functools.partial(_mla_attn_kernel, seq=seq, n_cast=len(cast)),
        grid=(batch, head_groups),
        in_specs=[qk_spec, qk_spec, vt_spec] + cast_in,
        out_specs=[vt_spec] + cast_out,
        out_shape=[jax.ShapeDtypeStruct((batch, MLA_HEADS * V_HEAD, seq), BF16)] + cast_shapes,
        compiler_params=_params("parallel", "parallel"),
        name="mla_attn",
    )(q.reshape(batch, seq, MLA_HEADS * HEAD_W), k.reshape(batch, seq, MLA_HEADS * HEAD_W), vt,
      *[w for w, _ in cast])


def _rope_tables(seq):
    pos = jnp.arange(seq, dtype=F32)
    inv_freq = ROPE_THETA ** (-jnp.arange(0, QK_ROPE, 2, dtype=F32) / QK_ROPE)
    ang = pos[:, None] * inv_freq[None, :]
    reps = LANES // (QK_ROPE // 2)
    return jnp.tile(jnp.cos(ang), (1, reps)), jnp.tile(jnp.sin(ang), (1, reps))


def _rotate_half_cols(w):
    half = w.shape[-1] // 2
    return jnp.concatenate([-w[..., half:], w[..., :half]], axis=-1)


def _mla_attention(x, w_dq, g_q, w_uq, w_dkv, g_kv, w_uk, w_uv, batch, seq, cast=()):
    wq = w_uq.reshape(Q_LORA, MLA_HEADS, QK_NOPE + QK_ROPE)
    wq_pe = wq[..., QK_NOPE:]
    wuq = jnp.concatenate(
        [wq[..., :QK_NOPE].reshape(Q_LORA, -1), wq_pe.reshape(Q_LORA, -1),
         _rotate_half_cols(wq_pe).reshape(Q_LORA, -1)], axis=1).astype(BF16)
    wk_pe = w_dkv[:, KV_LORA:]
    wk_pe_rot = _rotate_half_cols(wk_pe)
    wkpe = jnp.concatenate([wk_pe, wk_pe, wk_pe_rot, wk_pe_rot], axis=1).astype(BF16)
    wuvt = w_uv.reshape(KV_LORA, MLA_HEADS * V_HEAD).T.astype(BF16)
    cos, sin = _rope_tables(seq)
    q, k, vt = _mla_proj(x, cos, sin, w_dq.astype(BF16), g_q[None], wuq, w_dkv[:, :KV_LORA].astype(BF16),
                         g_kv[None], wkpe, w_uk.reshape(KV_LORA, -1).astype(BF16), wuvt, batch, seq)
    return _mla_attn(q, k, vt, batch, seq, cast=cast)


def kernel(x, sc_w_in, sc_conv_w, sc_w_out, mla_w_dq, mla_g_q, mla_w_uq, mla_w_dkv, mla_g_kv, mla_w_uk, mla_w_uv, mla_w_o, cf_w_pw1, cf_b_pw1, cf_dw_w, cf_dw_b, cf_norm_g, cf_norm_b, cf_w_pw2, cf_b_pw2, ff_w1, ff_w2, ln_mix_g, ln_mix_b, ln_ff_g, ln_ff_b):
    batch, seq, d = x.shape
    assert d == D_MODEL and seq % STEP_ROWS == 0
    h = x.reshape(batch * seq, d)

    calls = []
    for i in range(DEPTH):
        m, j = i % N_MIXERS, i // N_MIXERS
        mlp_w = [(ff_w1, i), (ff_w2, i)]
        if m == 1:
            calls.append(("attention", i, [], True))
            calls.append(("attn_out+mlp", i, mlp_w, False))
        elif m == 0 and i > 0:
            calls.append(("sconv+mlp", i, [(sc_w_in, j), (sc_w_out, j)] + mlp_w, False))
        else:
            mixer_w = [(sc_w_in, j), (sc_w_out, j)] if m == 0 else [(cf_w_pw1, j), (cf_w_pw2, j)]
            calls.append(("sconv" if m == 0 else "conf", i, mixer_w, True))
            calls.append(("mlp", i, mlp_w, True))
    jobs = [[] for _ in calls]
    source = [None] * len(calls)
    carrier = None
    for c, (_, _, needed, can_carry) in enumerate(calls):
        if needed and carrier is not None:
            source[c] = (carrier, len(jobs[carrier]))
            jobs[carrier].extend(needed)
        if can_carry:
            carrier = c
    casted = [None] * len(calls)

    def bf16_weights(c):
        needed = calls[c][2]
        if source[c] is None:
            return [w[layer].astype(BF16) for w, layer in needed]
        carrier, first = source[c]
        return casted[carrier][first:first + len(needed)]

    attn = None
    for c, (kind, i, _, _) in enumerate(calls):
        j = i // N_MIXERS
        g, b = ln_mix_g[i][None], ln_mix_b[i][None]
        weights = bf16_weights(c)
        mixer = {}
        if kind.startswith("sconv"):
            mixer = _sconv_mixer(weights[0], sc_conv_w[j], weights[1], g, b, seq)
        elif kind.startswith("conf"):
            mixer = _conf_mixer(weights[0], cf_b_pw1[j][None], cf_dw_w[j], cf_dw_b[j][None], cf_norm_g[j][None],
                                cf_norm_b[j][None], weights[1], cf_b_pw2[j][None], g, b, seq)
        elif kind.startswith("attn_out"):
            mixer = _attn_out_mixer(attn, mla_w_o[j].astype(BF16), g, b, seq)
        if kind == "attention":
            attn, *casted[c] = _mla_attention(h, mla_w_dq[j], mla_g_q[j], mla_w_uq[j], mla_w_dkv[j], mla_g_kv[j],
                                              mla_w_uk[j], mla_w_uv[j], batch, seq, cast=jobs[c])
            continue
        mlp = (*weights[-2:], ln_ff_g[i][None], ln_ff_b[i][None]) if kind.endswith("mlp") else None
        h, *casted[c] = _sublayer_call(kind.replace("+", "_") + "_ln", h, mlp=mlp, cast=jobs[c], **mixer)
    return h.reshape(batch, seq, d)
```

```python
import functools

import jax
import jax.numpy as jnp
from jax import lax
from jax.experimental import pallas as pl
from jax.experimental.pallas import tpu as pltpu

D_MODEL = 1024
DEPTH = 4
CHUNK = 64
N_MIXERS = 3
ALPHA = (2.0 * DEPTH) ** 0.25
LN_EPS = 1e-5
RMS_EPS = 1e-6
SC_WIDTH = 3
MLA_HEADS = 8
QK_NOPE = 128
QK_ROPE = 64
V_HEAD = 128
Q_LORA = 3 * D_MODEL // 8
KV_LORA = D_MODEL // 4
ROPE_THETA = 10000.0
CONF_WIDTH = 31
D_FF = 4 * D_MODEL

LANES = 128
SUBLANES = 8
N_SLABS = D_MODEL // LANES
HEAD_W = 2 * LANES
VMEM_LIMIT = 60 * 1024 * 1024

SUB_ROWS = 512
SUBTILES = 2
STEP_ROWS = SUBTILES * SUB_ROWS
FINE_SUB_ROWS = 256
ATT_TQ = 256
ATT_HEADS = 4
SC_HALO = 8
CF_HALO = 32
CONV_ROWS = 128
FF_CHUNKS = 4
TAIL_PIECES = 2
IN_CHUNK = 256
PROJ_CHUNK = 256

BF16 = jnp.bfloat16
F32 = jnp.float32
LOG2_E = 1.4426950408889634
NT_DIMS = (((1,), (1,)), ((), ()))
TN_DIMS = (((0,), (0,)), ((), ()))


def _dot(a, b):
    return jnp.dot(a, b, preferred_element_type=F32)


def _layer_norm(z, g, b):
    mu = jnp.mean(z, axis=-1, keepdims=True)
    zc = z - mu
    var = jnp.mean(zc * zc, axis=-1, keepdims=True)
    return zc * lax.rsqrt(var + LN_EPS) * g + b


def _rms_norm(z, g):
    return z * lax.rsqrt(jnp.mean(z * z, axis=-1, keepdims=True) + RMS_EPS) * g


def _sub(t):
    return slice(t * SUB_ROWS, (t + 1) * SUB_ROWS)


def _const_spec(shape):
    nd = len(shape)
    return pl.BlockSpec(shape, lambda *_: (0,) * nd, pipeline_mode=pl.Buffered(1))


def _row_spec(width):
    return pl.BlockSpec((STEP_ROWS, width), lambda i: (i, 0))


def _params(*sem):
    return pltpu.CompilerParams(dimension_semantics=sem, vmem_limit_bytes=VMEM_LIMIT)


def _cast_specs(jobs, n_steps, step=lambda i: i):
    in_specs, out_specs, shapes = [], [], []
    for w, layer in jobs:
        _, rows, cols = w.shape
        assert rows % n_steps == 0 and (rows // n_steps) % (2 * SUBLANES) == 0
        in_specs.append(pl.BlockSpec((None, rows // n_steps, cols),
                                     lambda *idx, layer=layer: (layer, step(*idx), 0)))
        out_specs.append(pl.BlockSpec((rows // n_steps, cols), lambda *idx: (step(*idx), 0)))
        shapes.append(jax.ShapeDtypeStruct((rows, cols), BF16))
    return in_specs, out_specs, shapes


def _cast_blocks(src_refs, dst_refs):
    for src, dst in zip(src_refs, dst_refs):
        dst[...] = src[...].astype(BF16)


def _mlp_stage(x_ref, w1_ref, w2_ref, g_ref, b_ref, o_ref, *, rows_per):
    fc = D_FF // FF_CHUNKS
    n_sub = STEP_ROWS // rows_per

    def sub(t):
        return slice(t * rows_per, (t + 1) * rows_per)

    def finish(rows, y):
        o_ref[rows, :] = _layer_norm(ALPHA * x_ref[rows, :] + y, g_ref[...], b_ref[...])

    pending = None
    for t in range(n_sub):
        xb = x_ref[sub(t), :].astype(BF16)
        y = None
        for c in range(FF_CHUNKS):
            h = _dot(xb, w1_ref[:, c * fc:(c + 1) * fc])
            h = jnp.maximum(h, 0.0)
            h = (h * h).astype(BF16)
            if t == n_sub - 1 and c == FF_CHUNKS - 1:
                piece = rows_per // TAIL_PIECES
                for r in range(TAIL_PIECES):
                    local = slice(r * piece, (r + 1) * piece)
                    y_piece = y[local, :] + _dot(h[local, :], w2_ref[c * fc:(c + 1) * fc, :])
                    finish(slice(t * rows_per + r * piece, t * rows_per + (r + 1) * piece), y_piece)
                break
            yc = _dot(h, w2_ref[c * fc:(c + 1) * fc, :])
            y = yc if y is None else y + yc
            if c == 0 and pending is not None:
                finish(*pending)
        pending = (sub(t), y)


def _sconv_stage(x_ref, win_ref, cw_ref, wout_ref, g_ref, b_ref, o_ref, ubuf, gbuf, *, steps_per_seq):
    rows_per, n_sub, sub = SUB_ROWS, SUBTILES, _sub

    @pl.when(pl.program_id(0) % steps_per_seq == 0)
    def _():
        ubuf[:, 0:SC_HALO, :] = jnp.zeros((N_SLABS, SC_HALO, LANES), F32)

    slabs_per_chunk = IN_CHUNK // LANES
    b_gates = []
    for t in range(n_sub):
        xb = x_ref[sub(t), :].astype(BF16)
        slabs = []
        for c in range(D_MODEL // IN_CHUNK):
            b_cols, c_cols, h_cols = (slice(part * D_MODEL + c * IN_CHUNK, part * D_MODEL + (c + 1) * IN_CHUNK)
                                      for part in range(3))
            b_gate = _dot(xb, win_ref[:, b_cols])
            u = _dot(xb, win_ref[:, c_cols]) * _dot(xb, win_ref[:, h_cols])
            for i in range(slabs_per_chunk):
                lanes = slice(i * LANES, (i + 1) * LANES)
                slabs.append(b_gate[:, lanes])
                ubuf[c * slabs_per_chunk + i, SC_HALO + t * rows_per:SC_HALO + (t + 1) * rows_per, :] = u[:, lanes]
        b_gates.append(slabs)

    ys = []
    for t in range(n_sub):
        for s in range(N_SLABS):
            cols = slice(s * LANES, (s + 1) * LANES)
            conv = None
            for k in range(SC_WIDTH):
                start = SC_HALO + t * rows_per - (SC_WIDTH - 1) + k
                term = cw_ref[k:k + 1, cols] * ubuf[s, start:start + rows_per, :]
                conv = term if conv is None else conv + term
            gbuf[sub(t), cols] = (b_gates[t][s] * conv).astype(BF16)
        ys.append(_dot(gbuf[sub(t), :], wout_ref[...]))

    for t in range(n_sub):
        o_ref[sub(t), :] = _layer_norm(ALPHA * x_ref[sub(t), :] + ys[t], g_ref[...], b_ref[...])

    ubuf[:, 0:SC_HALO, :] = ubuf[:, STEP_ROWS:STEP_ROWS + SC_HALO, :]


def _conf_stage(x_ref, w1_ref, b1_ref, dww_ref, dwb_ref, ng_ref, nb_ref, w2_ref, b2_ref, g_ref, b_ref,
                o_ref, hbuf, cbuf, *, steps_per_seq):
    @pl.when(pl.program_id(0) % steps_per_seq == 0)
    def _():
        hbuf[:, 0:CF_HALO, :] = jnp.zeros((N_SLABS, CF_HALO, LANES), F32)

    slabs_per_chunk = IN_CHUNK // LANES
    for t in range(SUBTILES):
        xb = x_ref[_sub(t), :].astype(BF16)
        for c in range(D_MODEL // IN_CHUNK):
            a_cols = slice(c * IN_CHUNK, (c + 1) * IN_CHUNK)
            g_cols = slice(D_MODEL + c * IN_CHUNK, D_MODEL + (c + 1) * IN_CHUNK)
            a = _dot(xb, w1_ref[:, a_cols]) + b1_ref[:, a_cols]
            gate = _dot(xb, w1_ref[:, g_cols]) + b1_ref[:, g_cols]
            h = a * jax.nn.sigmoid(gate)
            for i in range(slabs_per_chunk):
                hbuf[c * slabs_per_chunk + i, CF_HALO + t * SUB_ROWS:CF_HALO + (t + 1) * SUB_ROWS, :] = (
                    h[:, i * LANES:(i + 1) * LANES])

    for t in range(SUBTILES):
        for s in range(N_SLABS):
            cols = slice(s * LANES, (s + 1) * LANES)
            for r0 in range(t * SUB_ROWS, (t + 1) * SUB_ROWS, CONV_ROWS):
                acc = jnp.zeros((CONV_ROWS, LANES), F32) + dwb_ref[:, cols]
                for k in range(CONF_WIDTH):
                    start = CF_HALO + r0 - (CONF_WIDTH - 1) + k
                    acc = acc + dww_ref[k:k + 1, cols] * hbuf[s, start:start + CONV_ROWS, :]
                cbuf[r0:r0 + CONV_ROWS, cols] = acc
        hn = _layer_norm(cbuf[_sub(t), :], ng_ref[...], nb_ref[...])
        hs = (hn * jax.nn.sigmoid(hn)).astype(BF16)
        y = _dot(hs, w2_ref[...]) + b2_ref[...]
        o_ref[_sub(t), :] = _layer_norm(ALPHA * x_ref[_sub(t), :] + y, g_ref[...], b_ref[...])

    hbuf[:, 0:CF_HALO, :] = hbuf[:, STEP_ROWS:STEP_ROWS + CF_HALO, :]


def _attn_out_stage(x_ref, ot_ref, wo_ref, g_ref, b_ref, o_ref):
    ys = [lax.dot_general(ot_ref[0, :, _sub(t)], wo_ref[...], TN_DIMS, preferred_element_type=F32)
          for t in range(SUBTILES)]
    for t in range(SUBTILES):
        o_ref[_sub(t), :] = _layer_norm(ALPHA * x_ref[_sub(t), :] + ys[t], g_ref[...], b_ref[...])


def _sublayer_kernel(x_ref, *refs, mixer, n_mixer_in, with_mlp, mlp_sub_rows, n_cast):
    mixer_in, refs = refs[:n_mixer_in], refs[n_mixer_in:]
    n_mlp_in = 4 if with_mlp else 0
    mlp_in, refs = refs[:n_mlp_in], refs[n_mlp_in:]
    cast_src, o_ref, cast_dst, scratch = refs[:n_cast], refs[n_cast], refs[n_cast + 1:2 * n_cast + 1], refs[2 * n_cast + 1:]
    _cast_blocks(cast_src, cast_dst)
    if mixer is not None:
        mixer(x_ref, *mixer_in, o_ref, *scratch)
    if with_mlp:
        _mlp_stage(x_ref if mixer is None else o_ref, *mlp_in, o_ref, rows_per=mlp_sub_rows)


def _sublayer_call(name, x, mixer=None, mixer_args=(), mixer_specs=(), scratch=(), mlp=None, cast=(),
                   mlp_sub_rows=FINE_SUB_ROWS):
    n = x.shape[0]
    n_steps = n // STEP_ROWS
    vec = _const_spec((1, D_MODEL))
    mlp_specs = [] if mlp is None else [_const_spec((D_MODEL, D_FF)), _const_spec((D_FF, D_MODEL)), vec, vec]
    cast_in, cast_out, cast_shapes = _cast_specs(cast, n_steps)
    return pl.pallas_call(
        functools.partial(_sublayer_kernel, mixer=mixer, n_mixer_in=len(mixer_args),
                          with_mlp=mlp is not None, mlp_sub_rows=mlp_sub_rows, n_cast=len(cast)),
        grid=(n_steps,),
        in_specs=[_row_spec(D_MODEL)] + list(mixer_specs) + mlp_specs + cast_in,
        out_specs=[_row_spec(D_MODEL)] + cast_out,
        out_shape=[jax.ShapeDtypeStruct((n, D_MODEL), F32)] + cast_shapes,
        scratch_shapes=list(scratch),
        compiler_params=_params("parallel" if mixer is None else "arbitrary"),
        name=name,
    )(x, *mixer_args, *(mlp or ()), *[w for w, _ in cast])


def _sconv_mixer(w_in, conv_w, w_out, g, b, seq):
    vec = _const_spec((1, D_MODEL))
    return dict(
        mixer=functools.partial(_sconv_stage, steps_per_seq=seq // STEP_ROWS),
        mixer_args=(w_in, conv_w, w_out, g, b),
        mixer_specs=(_const_spec((D_MODEL, 3 * D_MODEL)), _const_spec((SC_WIDTH, D_MODEL)),
                     _const_spec((D_MODEL, D_MODEL)), vec, vec),
        scratch=(pltpu.VMEM((N_SLABS, SC_HALO + STEP_ROWS, LANES), F32), pltpu.VMEM((STEP_ROWS, D_MODEL), BF16)),
        mlp_sub_rows=SUB_ROWS)


def _conf_mixer(w_pw1, b_pw1, dw_w, dw_b, norm_g, norm_b, w_pw2, b_pw2, g, b, seq):
    vec = _const_spec((1, D_MODEL))
    return dict(
        mixer=functools.partial(_conf_stage, steps_per_seq=seq // STEP_ROWS),
        mixer_args=(w_pw1, b_pw1, dw_w, dw_b, norm_g, norm_b, w_pw2, b_pw2, g, b),
        mixer_specs=(_const_spec((D_MODEL, 2 * D_MODEL)), _const_spec((1, 2 * D_MODEL)),
                     _const_spec((CONF_WIDTH, D_MODEL)), vec, vec, vec, _const_spec((D_MODEL, D_MODEL)), vec, vec, vec),
        scratch=(pltpu.VMEM((N_SLABS, CF_HALO + STEP_ROWS, LANES), F32), pltpu.VMEM((STEP_ROWS, D_MODEL), F32)))


def _attn_out_mixer(ot, w_o, g, b, seq):
    sps = seq // STEP_ROWS
    hv = MLA_HEADS * V_HEAD
    vec = _const_spec((1, D_MODEL))
    return dict(
        mixer=_attn_out_stage,
        mixer_args=(ot, w_o, g, b),
        mixer_specs=(pl.BlockSpec((1, hv, STEP_ROWS), lambda i: (i // sps, 0, i % sps)),
                     _const_spec((hv, D_MODEL)), vec, vec))


def _mla_proj_kernel(x_ref, cos_ref, sin_ref, wdq_ref, gq_ref, wuq_ref, wdkv_ref, gkv_ref, wkpe_ref,
                     wuk_ref, wuvt_ref, q_ref, k_ref, vt_ref):
    nope_w = MLA_HEADS * QK_NOPE
    rope_w = MLA_HEADS * QK_ROPE
    scale = (QK_NOPE + QK_ROPE) ** -0.5 * LOG2_E
    first_half = lax.broadcasted_iota(jnp.int32, (SUB_ROWS, LANES), 1) < QK_ROPE
    for t in range(SUBTILES):
        xb = x_ref[_sub(t), :].astype(BF16)
        cos = cos_ref[_sub(t), :]
        sin = sin_ref[_sub(t), :]

        cq = _rms_norm(_dot(xb, wdq_ref[...]), gq_ref[...]).astype(BF16)
        heads_per_chunk = PROJ_CHUNK // QK_NOPE
        for c in range(nope_w // PROJ_CHUNK):
            q_nope = _dot(cq, wuq_ref[:, c * PROJ_CHUNK:(c + 1) * PROJ_CHUNK]) * scale
            for i in range(heads_per_chunk):
                hd = c * heads_per_chunk + i
                q_ref[_sub(t), hd * HEAD_W:hd * HEAD_W + LANES] = q_nope[:, i * LANES:(i + 1) * LANES].astype(BF16)
        pairs_per_chunk = PROJ_CHUNK // LANES
        for c in range(rope_w // PROJ_CHUNK):
            pe = _dot(cq, wuq_ref[:, nope_w + c * PROJ_CHUNK:nope_w + (c + 1) * PROJ_CHUNK])
            pe_rot = _dot(cq, wuq_ref[:, nope_w + rope_w + c * PROJ_CHUNK:nope_w + rope_w + (c + 1) * PROJ_CHUNK])
            for i in range(pairs_per_chunk):
                lanes = slice(i * LANES, (i + 1) * LANES)
                slab = ((pe[:, lanes] * cos + pe_rot[:, lanes] * sin) * scale).astype(BF16)
                pair = c * pairs_per_chunk + i
                for hd in (2 * pair, 2 * pair + 1):
                    q_ref[_sub(t), hd * HEAD_W + LANES:(hd + 1) * HEAD_W] = slab

        ckv = _rms_norm(_dot(xb, wdkv_ref[...]), gkv_ref[...]).astype(BF16)
        kpe2 = _dot(xb, wkpe_ref[...])
        k_pe = kpe2[:, 0:LANES] * cos + kpe2[:, LANES:2 * LANES] * sin
        k_pe_even = jnp.where(first_half, k_pe, 0.0).astype(BF16)
        k_pe_odd = jnp.where(first_half, 0.0, k_pe).astype(BF16)
        for c in range(nope_w // PROJ_CHUNK):
            k_nope = _dot(ckv, wuk_ref[:, c * PROJ_CHUNK:(c + 1) * PROJ_CHUNK])
            for i in range(heads_per_chunk):
                hd = c * heads_per_chunk + i
                k_ref[_sub(t), hd * HEAD_W:hd * HEAD_W + LANES] = k_nope[:, i * LANES:(i + 1) * LANES].astype(BF16)
                k_ref[_sub(t), hd * HEAD_W + LANES:(hd + 1) * HEAD_W] = k_pe_even if hd % 2 == 0 else k_pe_odd
        for c in range(MLA_HEADS * V_HEAD // PROJ_CHUNK):
            rows = slice(c * PROJ_CHUNK, (c + 1) * PROJ_CHUNK)
            vt = lax.dot_general(wuvt_ref[rows, :], ckv, NT_DIMS, preferred_element_type=F32)
            vt_ref[0, rows, _sub(t)] = vt.astype(BF16)


def _mla_proj(x, cos, sin, wdq, gq, wuq, wdkv, gkv, wkpe, wuk, wuvt, batch, seq):
    n = x.shape[0]
    sps = seq // STEP_ROWS
    hw = MLA_HEADS * HEAD_W
    hv = MLA_HEADS * V_HEAD
    pos_spec = pl.BlockSpec((STEP_ROWS, LANES), lambda i: (i % sps, 0))
    return pl.pallas_call(
        _mla_proj_kernel,
        grid=(n // STEP_ROWS,),
        in_specs=[_row_spec(D_MODEL), pos_spec, pos_spec,
                  _const_spec((D_MODEL, Q_LORA)), _const_spec((1, Q_LORA)),
                  _const_spec((Q_LORA, MLA_HEADS * (QK_NOPE + 2 * QK_ROPE))),
                  _const_spec((D_MODEL, KV_LORA)), _const_spec((1, KV_LORA)),
                  _const_spec((D_MODEL, 2 * LANES)),
                  _const_spec((KV_LORA, MLA_HEADS * QK_NOPE)), _const_spec((hv, KV_LORA))],
        out_specs=[_row_spec(hw), _row_spec(hw),
                   pl.BlockSpec((1, hv, STEP_ROWS), lambda i: (i // sps, 0, i % sps))],
        out_shape=[jax.ShapeDtypeStruct((n, hw), BF16), jax.ShapeDtypeStruct((n, hw), BF16),
                   jax.ShapeDtypeStruct((batch, hv, seq), BF16)],
        compiler_params=_params("parallel"),
        name="mla_proj",
    )(x, cos, sin, wdq, gq, wuq, wdkv, gkv, wkpe, wuk, wuvt)


def _mla_attn_kernel(q_ref, k_ref, vt_ref, *refs, seq, n_cast):
    cast_src, o_ref, cast_dst = refs[:n_cast], refs[n_cast], refs[n_cast + 1:]
    _cast_blocks(cast_src, cast_dst)
    key_chunk = lax.broadcasted_iota(jnp.int32, (ATT_TQ, ATT_TQ), 0) // CHUNK
    qry_chunk = lax.broadcasted_iota(jnp.int32, (ATT_TQ, ATT_TQ), 1) // CHUNK
    diag_allowed = key_chunk <= qry_chunk

    def scores(hd, qi):
        lo, hi = qi * ATT_TQ, (qi + 1) * ATT_TQ
        hcols = slice(hd * HEAD_W, (hd + 1) * HEAD_W)
        q = q_ref[0, lo:hi, hcols]
        s_diag = lax.dot_general(k_ref[0, lo:hi, hcols], q, NT_DIMS, preferred_element_type=F32)
        s_diag = jnp.where(diag_allowed, s_diag, -1e30)
        m = jnp.max(s_diag, axis=0, keepdims=True)
        s_off = None
        if qi > 0:
            s_off = lax.dot_general(k_ref[0, 0:lo, hcols], q, NT_DIMS, preferred_element_type=F32)
            m = jnp.maximum(m, jnp.max(s_off, axis=0, keepdims=True))
        return s_diag, s_off, m

    def finish(hd, qi, s_diag, s_off, m):
        lo, hi = qi * ATT_TQ, (qi + 1) * ATT_TQ
        vrows = slice(hd * V_HEAD, (hd + 1) * V_HEAD)
        p_diag = jnp.exp2(s_diag - m)
        l = jnp.sum(p_diag, axis=0, keepdims=True)
        acc = _dot(vt_ref[0, vrows, lo:hi], p_diag.astype(BF16))
        if qi > 0:
            p_off = jnp.exp2(s_off - m)
            l = l + jnp.sum(p_off, axis=0, keepdims=True)
            acc = acc + _dot(vt_ref[0, vrows, 0:lo], p_off.astype(BF16))
        o_ref[0, vrows, lo:hi] = (acc / l).astype(BF16)

    items = [(hd, qi) for qi in range(seq // ATT_TQ) for hd in range(ATT_HEADS)]
    pending = scores(*items[0])
    for n, item in enumerate(items):
        nxt = scores(*items[n + 1]) if n + 1 < len(items) else None
        finish(*item, *pending)
        pending = nxt


def _mla_attn(q, k, vt, batch, seq, cast=()):
    assert ATT_TQ % CHUNK == 0 and seq % ATT_TQ == 0 and MLA_HEADS % ATT_HEADS == 0
    head_groups = MLA_HEADS // ATT_HEADS
    qk_spec = pl.BlockSpec((1, seq, ATT_HEADS * HEAD_W), lambda b, h: (b, 0, h))
    vt_spec = pl.BlockSpec((1, ATT_HEADS * V_HEAD, seq), lambda b, h: (b, h, 0))
    cast_in, cast_out, cast_shapes = _cast_specs(cast, batch * head_groups, lambda b, h: b * head_groups + h)
    return pl.pallas_call(
        functools.partial(_mla_attn_kernel, seq=seq, n_cast=len(cast)),
        grid=(batch, head_groups),
        in_specs=[qk_spec, qk_spec, vt_spec] + cast_in,
        out_specs=[vt_spec] + cast_out,
        out_shape=[jax.ShapeDtypeStruct((batch, MLA_HEADS * V_HEAD, seq), BF16)] + cast_shapes,
        compiler_params=_params("parallel", "parallel"),
        name="mla_attn",
    )(q.reshape(batch, seq, MLA_HEADS * HEAD_W), k.reshape(batch, seq, MLA_HEADS * HEAD_W), vt,
      *[w for w, _ in cast])


def _rope_tables(seq):
    pos = jnp.arange(seq, dtype=F32)
    inv_freq = ROPE_THETA ** (-jnp.arange(0, QK_ROPE, 2, dtype=F32) / QK_ROPE)
    ang = pos[:, None] * inv_freq[None, :]
    reps = LANES // (QK_ROPE // 2)
    return jnp.tile(jnp.cos(ang), (1, reps)), jnp.tile(jnp.sin(ang), (1, reps))


def _rotate_half_cols(w):
    half = w.shape[-1] // 2
    return jnp.concatenate([-w[..., half:], w[..., :half]], axis=-1)


def _mla_attention(x, w_dq, g_q, w_uq, w_dkv, g_kv, w_uk, w_uv, batch, seq, cast=()):
    wq = w_uq.reshape(Q_LORA, MLA_HEADS, QK_NOPE + QK_ROPE)
    wq_pe = wq[..., QK_NOPE:]
    wuq = jnp.concatenate(
        [wq[..., :QK_NOPE].reshape(Q_LORA, -1), wq_pe.reshape(Q_LORA, -1),
         _rotate_half_cols(wq_pe).reshape(Q_LORA, -1)], axis=1).astype(BF16)
    wk_pe = w_dkv[:, KV_LORA:]
    wk_pe_rot = _rotate_half_cols(wk_pe)
    wkpe = jnp.concatenate([wk_pe, wk_pe, wk_pe_rot, wk_pe_rot], axis=1).astype(BF16)
    wuvt = w_uv.reshape(KV_LORA, MLA_HEADS * V_HEAD).T.astype(BF16)
    cos, sin = _rope_tables(seq)
    q, k, vt = _mla_proj(x, cos, sin, w_dq.astype(BF16), g_q[None], wuq, w_dkv[:, :KV_LORA].astype(BF16),
                         g_kv[None], wkpe, w_uk.reshape(KV_LORA, -1).astype(BF16), wuvt, batch, seq)
    return _mla_attn(q, k, vt, batch, seq, cast=cast)


def kernel(x, sc_w_in, sc_conv_w, sc_w_out, mla_w_dq, mla_g_q, mla_w_uq, mla_w_dkv, mla_g_kv, mla_w_uk, mla_w_uv, mla_w_o, cf_w_pw1, cf_b_pw1, cf_dw_w, cf_dw_b, cf_norm_g, cf_norm_b, cf_w_pw2, cf_b_pw2, ff_w1, ff_w2, ln_mix_g, ln_mix_b, ln_ff_g, ln_ff_b):
    batch, seq, d = x.shape
    assert d == D_MODEL and seq % STEP_ROWS == 0
    h = x.reshape(batch * seq, d)

    calls = []
    for i in range(DEPTH):
        m, j = i % N_MIXERS, i // N_MIXERS
        mlp_w = [(ff_w1, i), (ff_w2, i)]
        if m == 1:
            calls.append(("attention", i, [], True))
            calls.append(("attn_out+mlp", i, mlp_w, False))
        elif m == 0 and i > 0:
            calls.append(("sconv+mlp", i, [(sc_w_in, j), (sc_w_out, j)] + mlp_w, False))
        else:
            mixer_w = [(sc_w_in, j), (sc_w_out, j)] if m == 0 else [(cf_w_pw1, j), (cf_w_pw2, j)]
            calls.append(("sconv" if m == 0 else "conf", i, mixer_w, True))
            calls.append(("mlp", i, mlp_w, True))
    jobs = [[] for _ in calls]
    source = [None] * len(calls)
    carrier = None
    for c, (_, _, needed, can_carry) in enumerate(calls):
        if needed and carrier is not None:
            source[c] = (carrier, len(jobs[carrier]))
            jobs[carrier].extend(needed)
        if can_carry:
            carrier = c
    casted = [None] * len(calls)

    def bf16_weights(c):
        needed = calls[c][2]
        if source[c] is None:
            return [w[layer].astype(BF16) for w, layer in needed]
        carrier, first = source[c]
        return casted[carrier][first:first + len(needed)]

    attn = None
    for c, (kind, i, _, _) in enumerate(calls):
        j = i // N_MIXERS
        g, b = ln_mix_g[i][None], ln_mix_b[i][None]
        weights = bf16_weights(c)
        mixer = {}
        if kind.startswith("sconv"):
            mixer = _sconv_mixer(weights[0], sc_conv_w[j], weights[1], g, b, seq)
        elif kind.startswith("conf"):
            mixer = _conf_mixer(weights[0], cf_b_pw1[j][None], cf_dw_w[j], cf_dw_b[j][None], cf_norm_g[j][None],
                                cf_norm_b[j][None], weights[1], cf_b_pw2[j][None], g, b, seq)
        elif kind.startswith("attn_out"):
            mixer = _attn_out_mixer(attn, mla_w_o[j].astype(BF16), g, b, seq)
        if kind == "attention":
            attn, *casted[c] = _mla_attention(h, mla_w_dq[j], mla_g_q[j], mla_w_uq[j], mla_w_dkv[j], mla_g_kv[j],
                                              mla_w_uk[j], mla_w_uv[j], batch, seq, cast=jobs[c])
            continue
        mlp = (*weights[-2:], ln_ff_g[i][None], ln_ff_b[i][None]) if kind.endswith("mlp") else None
        if kind == "mlp" and c + 1 < len(calls) and calls[c + 1][0] == "sconv+mlp":
            mixer = dict(mlp_sub_rows=SUB_ROWS)
        h, *casted[c] = _sublayer_call(kind.replace("+", "_") + "_ln", h, mlp=mlp, cast=jobs[c], **mixer)
    return h.reshape(batch, seq, d)
```

```python
import functools

import jax
import jax.numpy as jnp
from jax import lax
from jax.experimental import pallas as pl
from jax.experimental.pallas import tpu as pltpu

D_MODEL = 1024
DEPTH = 4
CHUNK = 64
N_MIXERS = 3
ALPHA = (2.0 * DEPTH) ** 0.25
LN_EPS = 1e-5
RMS_EPS = 1e-6
SC_WIDTH = 3
MLA_HEADS = 8
QK_NOPE = 128
QK_ROPE = 64
V_HEAD = 128
Q_LORA = 3 * D_MODEL // 8
KV_LORA = D_MODEL // 4
ROPE_THETA = 10000.0
CONF_WIDTH = 31
D_FF = 4 * D_MODEL

LANES = 128
SUBLANES = 8
N_SLABS = D_MODEL // LANES
HEAD_W = 2 * LANES
VMEM_LIMIT = 60 * 1024 * 1024

SUB_ROWS = 512
SUBTILES = 2
STEP_ROWS = SUBTILES * SUB_ROWS
FINE_SUB_ROWS = 256
ATT_TQ = 256
ATT_HEADS = 4
SC_HALO = 8
CF_HALO = 32
CONV_ROWS = 128
FF_CHUNKS = 4
TAIL_PIECES = 2
IN_CHUNK = 256
PROJ_CHUNK = 256

BF16 = jnp.bfloat16
F32 = jnp.float32
LOG2_E = 1.4426950408889634
NT_DIMS = (((1,), (1,)), ((), ()))
TN_DIMS = (((0,), (0,)), ((), ()))


def _dot(a, b):
    return jnp.dot(a, b, preferred_element_type=F32)


def _layer_norm(z, g, b):
    mu = jnp.mean(z, axis=-1, keepdims=True)
    zc = z - mu
    var = jnp.mean(zc * zc, axis=-1, keepdims=True)
    return zc * lax.rsqrt(var + LN_EPS) * g + b


def _rms_norm(z, g):
    return z * lax.rsqrt(jnp.mean(z * z, axis=-1, keepdims=True) + RMS_EPS) * g


def _sub(t):
    return slice(t * SUB_ROWS, (t + 1) * SUB_ROWS)


def _const_spec(shape):
    nd = len(shape)
    return pl.BlockSpec(shape, lambda *_: (0,) * nd, pipeline_mode=pl.Buffered(1))


def _row_spec(width):
    return pl.BlockSpec((STEP_ROWS, width), lambda i: (i, 0))


def _params(*sem):
    return pltpu.CompilerParams(dimension_semantics=sem, vmem_limit_bytes=VMEM_LIMIT)


def _cast_specs(jobs, n_steps, step=lambda i: i):
    in_specs, out_specs, shapes = [], [], []
    for w, layer in jobs:
        _, rows, cols = w.shape
        assert rows % n_steps == 0 and (rows // n_steps) % (2 * SUBLANES) == 0
        in_specs.append(pl.BlockSpec((None, rows // n_steps, cols),
                                     lambda *idx, layer=layer: (layer, step(*idx), 0)))
        out_specs.append(pl.BlockSpec((rows // n_steps, cols), lambda *idx: (step(*idx), 0)))
        shapes.append(jax.ShapeDtypeStruct((rows, cols), BF16))
    return in_specs, out_specs, shapes


def _cast_blocks(src_refs, dst_refs):
    for src, dst in zip(src_refs, dst_refs):
        dst[...] = src[...].astype(BF16)


def _mlp_stage(x_ref, w1_ref, w2_ref, g_ref, b_ref, o_ref, *, rows_per):
    fc = D_FF // FF_CHUNKS
    n_sub = STEP_ROWS // rows_per

    def sub(t):
        return slice(t * rows_per, (t + 1) * rows_per)

    def finish(rows, y):
        o_ref[rows, :] = _layer_norm(ALPHA * x_ref[rows, :] + y, g_ref[...], b_ref[...])

    pending = None
    for t in range(n_sub):
        xb = x_ref[sub(t), :].astype(BF16)
        y = None
        for c in range(FF_CHUNKS):
            h = _dot(xb, w1_ref[:, c * fc:(c + 1) * fc])
            h = jnp.maximum(h, 0.0)
            h = (h * h).astype(BF16)
            if t == n_sub - 1 and c == FF_CHUNKS - 1:
                piece = rows_per // TAIL_PIECES
                for r in range(TAIL_PIECES):
                    local = slice(r * piece, (r + 1) * piece)
                    y_piece = y[local, :] + _dot(h[local, :], w2_ref[c * fc:(c + 1) * fc, :])
                    finish(slice(t * rows_per + r * piece, t * rows_per + (r + 1) * piece), y_piece)
                break
            yc = _dot(h, w2_ref[c * fc:(c + 1) * fc, :])
            y = yc if y is None else y + yc
            if c == 0 and pending is not None:
                finish(*pending)
        pending = (sub(t), y)


def _sconv_stage(x_ref, win_ref, cw_ref, wout_ref, g_ref, b_ref, o_ref, ubuf, gbuf, *, steps_per_seq):
    rows_per, n_sub, sub = SUB_ROWS, SUBTILES, _sub

    @pl.when(pl.program_id(0) % steps_per_seq == 0)
    def _():
        ubuf[:, 0:SC_HALO, :] = jnp.zeros((N_SLABS, SC_HALO, LANES), F32)

    slabs_per_chunk = IN_CHUNK // LANES
    b_gates = []
    for t in range(n_sub):
        xb = x_ref[sub(t), :].astype(BF16)
        slabs = []
        for c in range(D_MODEL // IN_CHUNK):
            b_cols, c_cols, h_cols = (slice(part * D_MODEL + c * IN_CHUNK, part * D_MODEL + (c + 1) * IN_CHUNK)
                                      for part in range(3))
            b_gate = _dot(xb, win_ref[:, b_cols])
            u = _dot(xb, win_ref[:, c_cols]) * _dot(xb, win_ref[:, h_cols])
            for i in range(slabs_per_chunk):
                lanes = slice(i * LANES, (i + 1) * LANES)
                slabs.append(b_gate[:, lanes])
                ubuf[c * slabs_per_chunk + i, SC_HALO + t * rows_per:SC_HALO + (t + 1) * rows_per, :] = u[:, lanes]
        b_gates.append(slabs)

    ys = []
    for t in range(n_sub):
        for s in range(N_SLABS):
            cols = slice(s * LANES, (s + 1) * LANES)
            conv = None
            for k in range(SC_WIDTH):
                start = SC_HALO + t * rows_per - (SC_WIDTH - 1) + k
                term = cw_ref[k:k + 1, cols] * ubuf[s, start:start + rows_per, :]
                conv = term if conv is None else conv + term
            gbuf[sub(t), cols] = (b_gates[t][s] * conv).astype(BF16)
        ys.append(_dot(gbuf[sub(t), :], wout_ref[...]))

    for t in range(n_sub):
        o_ref[sub(t), :] = _layer_norm(ALPHA * x_ref[sub(t), :] + ys[t], g_ref[...], b_ref[...])

    ubuf[:, 0:SC_HALO, :] = ubuf[:, STEP_ROWS:STEP_ROWS + SC_HALO, :]


def _conf_stage(x_ref, w1_ref, b1_ref, dww_ref, dwb_ref, ng_ref, nb_ref, w2_ref, b2_ref, g_ref, b_ref,
                o_ref, hbuf, cbuf, *, steps_per_seq):
    @pl.when(pl.program_id(0) % steps_per_seq == 0)
    def _():
        hbuf[:, 0:CF_HALO, :] = jnp.zeros((N_SLABS, CF_HALO, LANES), F32)

    slabs_per_chunk = IN_CHUNK // LANES
    for t in range(SUBTILES):
        xb = x_ref[_sub(t), :].astype(BF16)
        for c in range(D_MODEL // IN_CHUNK):
            a_cols = slice(c * IN_CHUNK, (c + 1) * IN_CHUNK)
            g_cols = slice(D_MODEL + c * IN_CHUNK, D_MODEL + (c + 1) * IN_CHUNK)
            a = _dot(xb, w1_ref[:, a_cols]) + b1_ref[:, a_cols]
            gate = _dot(xb, w1_ref[:, g_cols]) + b1_ref[:, g_cols]
            h = a * jax.nn.sigmoid(gate)
            for i in range(slabs_per_chunk):
                hbuf[c * slabs_per_chunk + i, CF_HALO + t * SUB_ROWS:CF_HALO + (t + 1) * SUB_ROWS, :] = (
                    h[:, i * LANES:(i + 1) * LANES])

    for t in range(SUBTILES):
        for s in range(N_SLABS):
            cols = slice(s * LANES, (s + 1) * LANES)
            for r0 in range(t * SUB_ROWS, (t + 1) * SUB_ROWS, CONV_ROWS):
                acc = jnp.zeros((CONV_ROWS, LANES), F32) + dwb_ref[:, cols]
                for k in range(CONF_WIDTH):
                    start = CF_HALO + r0 - (CONF_WIDTH - 1) + k
                    acc = acc + dww_ref[k:k + 1, cols] * hbuf[s, start:start + CONV_ROWS, :]
                cbuf[r0:r0 + CONV_ROWS, cols] = acc
        hn = _layer_norm(cbuf[_sub(t), :], ng_ref[...], nb_ref[...])
        hs = (hn * jax.nn.sigmoid(hn)).astype(BF16)
        y = _dot(hs, w2_ref[...]) + b2_ref[...]
        o_ref[_sub(t), :] = _layer_norm(ALPHA * x_ref[_sub(t), :] + y, g_ref[...], b_ref[...])

    hbuf[:, 0:CF_HALO, :] = hbuf[:, STEP_ROWS:STEP_ROWS + CF_HALO, :]


def _attn_out_stage(x_ref, ot_ref, wo_ref, g_ref, b_ref, o_ref):
    ys = [lax.dot_general(ot_ref[0, :, _sub(t)], wo_ref[...], TN_DIMS, preferred_element_type=F32)
          for t in range(SUBTILES)]
    for t in range(SUBTILES):
        o_ref[_sub(t), :] = _layer_norm(ALPHA * x_ref[_sub(t), :] + ys[t], g_ref[...], b_ref[...])


def _sublayer_kernel(x_ref, *refs, mixer, n_mixer_in, with_mlp, mlp_sub_rows, n_cast):
    mixer_in, refs = refs[:n_mixer_in], refs[n_mixer_in:]
    n_mlp_in = 4 if with_mlp else 0
    mlp_in, refs = refs[:n_mlp_in], refs[n_mlp_in:]
    cast_src, o_ref, cast_dst, scratch = refs[:n_cast], refs[n_cast], refs[n_cast + 1:2 * n_cast + 1], refs[2 * n_cast + 1:]
    _cast_blocks(cast_src, cast_dst)
    if mixer is not None:
        mixer(x_ref, *mixer_in, o_ref, *scratch)
    if with_mlp:
        _mlp_stage(x_ref if mixer is None else o_ref, *mlp_in, o_ref, rows_per=mlp_sub_rows)


def _sublayer_call(name, x, mixer=None, mixer_args=(), mixer_specs=(), scratch=(), mlp=None, cast=(),
                   mlp_sub_rows=FINE_SUB_ROWS):
    n = x.shape[0]
    n_steps = n // STEP_ROWS
    vec = _const_spec((1, D_MODEL))
    mlp_specs = [] if mlp is None else [_const_spec((D_MODEL, D_FF)), _const_spec((D_FF, D_MODEL)), vec, vec]
    cast_in, cast_out, cast_shapes = _cast_specs(cast, n_steps)
    return pl.pallas_call(
        functools.partial(_sublayer_kernel, mixer=mixer, n_mixer_in=len(mixer_args),
                          with_mlp=mlp is not None, mlp_sub_rows=mlp_sub_rows, n_cast=len(cast)),
        grid=(n_steps,),
        in_specs=[_row_spec(D_MODEL)] + list(mixer_specs) + mlp_specs + cast_in,
        out_specs=[_row_spec(D_MODEL)] + cast_out,
        out_shape=[jax.ShapeDtypeStruct((n, D_MODEL), F32)] + cast_shapes,
        scratch_shapes=list(scratch),
        compiler_params=_params("parallel" if mixer is None else "arbitrary"),
        name=name,
    )(x, *mixer_args, *(mlp or ()), *[w for w, _ in cast])


def _sconv_mixer(w_in, conv_w, w_out, g, b, seq):
    vec = _const_spec((1, D_MODEL))
    return dict(
        mixer=functools.partial(_sconv_stage, steps_per_seq=seq // STEP_ROWS),
        mixer_args=(w_in, conv_w, w_out, g, b),
        mixer_specs=(_const_spec((D_MODEL, 3 * D_MODEL)), _const_spec((SC_WIDTH, D_MODEL)),
                     _const_spec((D_MODEL, D_MODEL)), vec, vec),
        scratch=(pltpu.VMEM((N_SLABS, SC_HALO + STEP_ROWS, LANES), F32), pltpu.VMEM((STEP_ROWS, D_MODEL), BF16)))


def _conf_mixer(w_pw1, b_pw1, dw_w, dw_b, norm_g, norm_b, w_pw2, b_pw2, g, b, seq):
    vec = _const_spec((1, D_MODEL))
    return dict(
        mixer=functools.partial(_conf_stage, steps_per_seq=seq // STEP_ROWS),
        mixer_args=(w_pw1, b_pw1, dw_w, dw_b, norm_g, norm_b, w_pw2, b_pw2, g, b),
        mixer_specs=(_const_spec((D_MODEL, 2 * D_MODEL)), _const_spec((1, 2 * D_MODEL)),
                     _const_spec((CONF_WIDTH, D_MODEL)), vec, vec, vec, _const_spec((D_MODEL, D_MODEL)), vec, vec, vec),
        scratch=(pltpu.VMEM((N_SLABS, CF_HALO + STEP_ROWS, LANES), F32), pltpu.VMEM((STEP_ROWS, D_MODEL), F32)))


def _attn_out_mixer(ot, w_o, g, b, seq):
    sps = seq // STEP_ROWS
    hv = MLA_HEADS * V_HEAD
    vec = _const_spec((1, D_MODEL))
    return dict(
        mixer=_attn_out_stage,
        mixer_args=(ot, w_o, g, b),
        mixer_specs=(pl.BlockSpec((1, hv, STEP_ROWS), lambda i: (i // sps, 0, i % sps)),
                     _const_spec((hv, D_MODEL)), vec, vec))


def _mla_proj_kernel(x_ref, cos_ref, sin_ref, wdq_ref, gq_ref, wuq_ref, wdkv_ref, gkv_ref, wkpe_ref,
                     wuk_ref, wuvt_ref, q_ref, k_ref, vt_ref):
    nope_w = MLA_HEADS * QK_NOPE
    rope_w = MLA_HEADS * QK_ROPE
    scale = (QK_NOPE + QK_ROPE) ** -0.5 * LOG2_E
    first_half = lax.broadcasted_iota(jnp.int32, (SUB_ROWS, LANES), 1) < QK_ROPE
    for t in range(SUBTILES):
        xb = x_ref[_sub(t), :].astype(BF16)
        cos = cos_ref[_sub(t), :]
        sin = sin_ref[_sub(t), :]

        cq = _rms_norm(_dot(xb, wdq_ref[...]), gq_ref[...]).astype(BF16)
        heads_per_chunk = PROJ_CHUNK // QK_NOPE
        for c in range(nope_w // PROJ_CHUNK):
            q_nope = _dot(cq, wuq_ref[:, c * PROJ_CHUNK:(c + 1) * PROJ_CHUNK]) * scale
            for i in range(heads_per_chunk):
                hd = c * heads_per_chunk + i
                q_ref[_sub(t), hd * HEAD_W:hd * HEAD_W + LANES] = q_nope[:, i * LANES:(i + 1) * LANES].astype(BF16)
        pairs_per_chunk = PROJ_CHUNK // LANES
        for c in range(rope_w // PROJ_CHUNK):
            pe = _dot(cq, wuq_ref[:, nope_w + c * PROJ_CHUNK:nope_w + (c + 1) * PROJ_CHUNK])
            pe_rot = _dot(cq, wuq_ref[:, nope_w + rope_w + c * PROJ_CHUNK:nope_w + rope_w + (c + 1) * PROJ_CHUNK])
            for i in range(pairs_per_chunk):
                lanes = slice(i * LANES, (i + 1) * LANES)
                slab = ((pe[:, lanes] * cos + pe_rot[:, lanes] * sin) * scale).astype(BF16)
                pair = c * pairs_per_chunk + i
                for hd in (2 * pair, 2 * pair + 1):
                    q_ref[_sub(t), hd * HEAD_W + LANES:(hd + 1) * HEAD_W] = slab

        ckv = _rms_norm(_dot(xb, wdkv_ref[...]), gkv_ref[...]).astype(BF16)
        kpe2 = _dot(xb, wkpe_ref[...])
        k_pe = kpe2[:, 0:LANES] * cos + kpe2[:, LANES:2 * LANES] * sin
        k_pe_even = jnp.where(first_half, k_pe, 0.0).astype(BF16)
        k_pe_odd = jnp.where(first_half, 0.0, k_pe).astype(BF16)
        for c in range(nope_w // PROJ_CHUNK):
            k_nope = _dot(ckv, wuk_ref[:, c * PROJ_CHUNK:(c + 1) * PROJ_CHUNK])
            for i in range(heads_per_chunk):
                hd = c * heads_per_chunk + i
                k_ref[_sub(t), hd * HEAD_W:hd * HEAD_W + LANES] = k_nope[:, i * LANES:(i + 1) * LANES].astype(BF16)
                k_ref[_sub(t), hd * HEAD_W + LANES:(hd + 1) * HEAD_W] = k_pe_even if hd % 2 == 0 else k_pe_odd
        for c in range(MLA_HEADS * V_HEAD // PROJ_CHUNK):
            rows = slice(c * PROJ_CHUNK, (c + 1) * PROJ_CHUNK)
            vt = lax.dot_general(wuvt_ref[rows, :], ckv, NT_DIMS, preferred_element_type=F32)
            vt_ref[0, rows, _sub(t)] = vt.astype(BF16)


def _mla_proj(x, cos, sin, wdq, gq, wuq, wdkv, gkv, wkpe, wuk, wuvt, batch, seq):
    n = x.shape[0]
    sps = seq // STEP_ROWS
    hw = MLA_HEADS * HEAD_W
    hv = MLA_HEADS * V_HEAD
    pos_spec = pl.BlockSpec((STEP_ROWS, LANES), lambda i: (i % sps, 0))
    return pl.pallas_call(
        _mla_proj_kernel,
        grid=(n // STEP_ROWS,),
        in_specs=[_row_spec(D_MODEL), pos_spec, pos_spec,
                  _const_spec((D_MODEL, Q_LORA)), _const_spec((1, Q_LORA)),
                  _const_spec((Q_LORA, MLA_HEADS * (QK_NOPE + 2 * QK_ROPE))),
                  _const_spec((D_MODEL, KV_LORA)), _const_spec((1, KV_LORA)),
                  _const_spec((D_MODEL, 2 * LANES)),
                  _const_spec((KV_LORA, MLA_HEADS * QK_NOPE)), _const_spec((hv, KV_LORA))],
        out_specs=[_row_spec(hw), _row_spec(hw),
                   pl.BlockSpec((1, hv, STEP_ROWS), lambda i: (i // sps, 0, i % sps))],
        out_shape=[jax.ShapeDtypeStruct((n, hw), BF16), jax.ShapeDtypeStruct((n, hw), BF16),
                   jax.ShapeDtypeStruct((batch, hv, seq), BF16)],
        compiler_params=_params("parallel"),
        name="mla_proj",
    )(x, cos, sin, wdq, gq, wuq, wdkv, gkv, wkpe, wuk, wuvt)


def _mla_attn_kernel(q_ref, k_ref, vt_ref, *refs, seq, n_cast):
    cast_src, o_ref, cast_dst = refs[:n_cast], refs[n_cast], refs[n_cast + 1:]
    _cast_blocks(cast_src, cast_dst)
    key_chunk = lax.broadcasted_iota(jnp.int32, (ATT_TQ, ATT_TQ), 0) // CHUNK
    qry_chunk = lax.broadcasted_iota(jnp.int32, (ATT_TQ, ATT_TQ), 1) // CHUNK
    diag_allowed = key_chunk <= qry_chunk

    def scores(hd, qi):
        lo, hi = qi * ATT_TQ, (qi + 1) * ATT_TQ
        hcols = slice(hd * HEAD_W, (hd + 1) * HEAD_W)
        q = q_ref[0, lo:hi, hcols]
        s_diag = lax.dot_general(k_ref[0, lo:hi, hcols], q, NT_DIMS, preferred_element_type=F32)
        s_diag = jnp.where(diag_allowed, s_diag, -1e30)
        m = jnp.max(s_diag, axis=0, keepdims=True)
        s_off = None
        if qi > 0:
            s_off = lax.dot_general(k_ref[0, 0:lo, hcols], q, NT_DIMS, preferred_element_type=F32)
            m = jnp.maximum(m, jnp.max(s_off, axis=0, keepdims=True))
        return s_diag, s_off, m

    def finish(hd, qi, s_diag, s_off, m):
        lo, hi = qi * ATT_TQ, (qi + 1) * ATT_TQ
        vrows = slice(hd * V_HEAD, (hd + 1) * V_HEAD)
        p_diag = jnp.exp2(s_diag - m)
        l = jnp.sum(p_diag, axis=0, keepdims=True)
        acc = _dot(vt_ref[0, vrows, lo:hi], p_diag.astype(BF16))
        if qi > 0:
            p_off = jnp.exp2(s_off - m)
            l = l + jnp.sum(p_off, axis=0, keepdims=True)
            acc = acc + _dot(vt_ref[0, vrows, 0:lo], p_off.astype(BF16))
        o_ref[0, vrows, lo:hi] = (acc / l).astype(BF16)

    items = [(hd, qi) for qi in range(seq // ATT_TQ) for hd in range(ATT_HEADS)]
    pending = scores(*items[0])
    for n, item in enumerate(items):
        nxt = scores(*items[n + 1]) if n + 1 < len(items) else None
        finish(*item, *pending)
        pending = nxt


def _mla_attn(q, k, vt, batch, seq, cast=()):
    assert ATT_TQ % CHUNK == 0 and seq % ATT_TQ == 0 and MLA_HEADS % ATT_HEADS == 0
    head_groups = MLA_HEADS // ATT_HEADS
    qk_spec = pl.BlockSpec((1, seq, ATT_HEADS * HEAD_W), lambda b, h: (b, 0, h))
    vt_spec = pl.BlockSpec((1, ATT_HEADS * V_HEAD, seq), lambda b, h: (b, h, 0))
    cast_in, cast_out, cast_shapes = _cast_specs(cast, batch * head_groups, lambda b, h: b * head_groups + h)
    return pl.pallas_call(
        functools.partial(_mla_attn_kernel, seq=seq, n_cast=len(cast)),
        grid=(batch, head_groups),
        in_specs=[qk_spec, qk_spec, vt_spec] + cast_in,
        out_specs=[vt_spec] + cast_out,
        out_shape=[jax.ShapeDtypeStruct((batch, MLA_HEADS * V_HEAD, seq), BF16)] + cast_shapes,
        compiler_params=_params("parallel", "parallel"),
        name="mla_attn",
    )(q.reshape(batch, seq, MLA_HEADS * HEAD_W), k.reshape(batch, seq, MLA_HEADS * HEAD_W), vt,
      *[w for w, _ in cast])


def _rope_tables(seq):
    pos = jnp.arange(seq, dtype=F32)
    inv_freq = ROPE_THETA ** (-jnp.arange(0, QK_ROPE, 2, dtype=F32) / QK_ROPE)
    ang = pos[:, None] * inv_freq[None, :]
    reps = LANES // (QK_ROPE // 2)
    return jnp.tile(jnp.cos(ang), (1, reps)), jnp.tile(jnp.sin(ang), (1, reps))


def _rotate_half_cols(w):
    half = w.shape[-1] // 2
    return jnp.concatenate([-w[..., half:], w[..., :half]], axis=-1)


def _mla_attention(x, w_dq, g_q, w_uq, w_dkv, g_kv, w_uk, w_uv, batch, seq, cast=()):
    wq = w_uq.reshape(Q_LORA, MLA_HEADS, QK_NOPE + QK_ROPE)
    wq_pe = wq[..., QK_NOPE:]
    wuq = jnp.concatenate(
        [wq[..., :QK_NOPE].reshape(Q_LORA, -1), wq_pe.reshape(Q_LORA, -1),
         _rotate_half_cols(wq_pe).reshape(Q_LORA, -1)], axis=1).astype(BF16)
    wk_pe = w_dkv[:, KV_LORA:]
    wk_pe_rot = _rotate_half_cols(wk_pe)
    wkpe = jnp.concatenate([wk_pe, wk_pe, wk_pe_rot, wk_pe_rot], axis=1).astype(BF16)
    wuvt = w_uv.reshape(KV_LORA, MLA_HEADS * V_HEAD).T.astype(BF16)
    cos, sin = _rope_tables(seq)
    q, k, vt = _mla_proj(x, cos, sin, w_dq.astype(BF16), g_q[None], wuq, w_dkv[:, :KV_LORA].astype(BF16),
                         g_kv[None], wkpe, w_uk.reshape(KV_LORA, -1).astype(BF16), wuvt, batch, seq)
    return _mla_attn(q, k, vt, batch, seq, cast=cast)


def kernel(x, sc_w_in, sc_conv_w, sc_w_out, mla_w_dq, mla_g_q, mla_w_uq, mla_w_dkv, mla_g_kv, mla_w_uk, mla_w_uv, mla_w_o, cf_w_pw1, cf_b_pw1, cf_dw_w, cf_dw_b, cf_norm_g, cf_norm_b, cf_w_pw2, cf_b_pw2, ff_w1, ff_w2, ln_mix_g, ln_mix_b, ln_ff_g, ln_ff_b):
    batch, seq, d = x.shape
    assert d == D_MODEL and seq % STEP_ROWS == 0
    h = x.reshape(batch * seq, d)

    calls = []
    for i in range(DEPTH):
        m, j = i % N_MIXERS, i // N_MIXERS
        mlp_w = [(ff_w1, i), (ff_w2, i)]
        if m == 1:
            calls.append(("attention", i, [], True))
            calls.append(("attn_out+mlp", i, mlp_w, False))
        elif m == 0 and i > 0:
            calls.append(("sconv+mlp", i, [(sc_w_in, j), (sc_w_out, j)] + mlp_w, False))
        else:
            mixer_w = [(sc_w_in, j), (sc_w_out, j)] if m == 0 else [(cf_w_pw1, j), (cf_w_pw2, j)]
            calls.append(("sconv" if m == 0 else "conf", i, mixer_w, True))
            calls.append(("mlp", i, mlp_w, True))
    jobs = [[] for _ in calls]
    source = [None] * len(calls)
    carrier = None
    for c, (_, _, needed, can_carry) in enumerate(calls):
        if needed and carrier is not None:
            source[c] = (carrier, len(jobs[carrier]))
            jobs[carrier].extend(needed)
        if can_carry:
            carrier = c
    casted = [None] * len(calls)

    def bf16_weights(c):
        needed = calls[c][2]
        if source[c] is None:
            return [w[layer].astype(BF16) for w, layer in needed]
        carrier, first = source[c]
        return casted[carrier][first:first + len(needed)]

    attn = None
    for c, (kind, i, _, _) in enumerate(calls):
        j = i // N_MIXERS
        g, b = ln_mix_g[i][None], ln_mix_b[i][None]
        weights = bf16_weights(c)
        mixer = {}
        if kind.startswith("sconv"):
            mixer = _sconv_mixer(weights[0], sc_conv_w[j], weights[1], g, b, seq)
        elif kind.startswith("conf"):
            mixer = _conf_mixer(weights[0], cf_b_pw1[j][None], cf_dw_w[j], cf_dw_b[j][None], cf_norm_g[j][None],
                                cf_norm_b[j][None], weights[1], cf_b_pw2[j][None], g, b, seq)
        elif kind.startswith("attn_out"):
            mixer = _attn_out_mixer(attn, mla_w_o[j].astype(BF16), g, b, seq)
        if kind == "attention":
            attn, *casted[c] = _mla_attention(h, mla_w_dq[j], mla_g_q[j], mla_w_uq[j], mla_w_dkv[j], mla_g_kv[j],
                                              mla_w_uk[j], mla_w_uv[j], batch, seq, cast=jobs[c])
            continue
        mlp = (*weights[-2:], ln_ff_g[i][None], ln_ff_b[i][None]) if kind.endswith("mlp") else None
        if kind == "mlp" and c + 1 < len(calls) and calls[c + 1][0] == "sconv+mlp":
            mixer = dict(mlp_sub_rows=SUB_ROWS)
        h, *casted[c] = _sublayer_call(kind.replace("+", "_") + "_ln", h, mlp=mlp, cast=jobs[c], **mixer)
    return h.reshape(batch, seq, d)
```

```python
import functools

import jax
import jax.numpy as jnp
from jax import lax
from jax.experimental import pallas as pl
from jax.experimental.pallas import tpu as pltpu

D_MODEL = 1024
DEPTH = 4
CHUNK = 64
N_MIXERS = 3
ALPHA = (2.0 * DEPTH) ** 0.25
LN_EPS = 1e-5
RMS_EPS = 1e-6
SC_WIDTH = 3
MLA_HEADS = 8
QK_NOPE = 128
QK_ROPE = 64
V_HEAD = 128
Q_LORA = 3 * D_MODEL // 8
KV_LORA = D_MODEL // 4
ROPE_THETA = 10000.0
CONF_WIDTH = 31
D_FF = 4 * D_MODEL

LANES = 128
SUBLANES = 8
N_SLABS = D_MODEL // LANES
HEAD_W = 2 * LANES
VMEM_LIMIT = 60 * 1024 * 1024

SUB_ROWS = 512
SUBTILES = 2
STEP_ROWS = SUBTILES * SUB_ROWS
FINE_SUB_ROWS = 256
ATT_TQ = 256
ATT_HEADS = 4
SC_HALO = 8
CF_HALO = 32
CONV_ROWS = 128
FF_CHUNKS = 4
TAIL_PIECES = 2
IN_CHUNK = 256
PROJ_CHUNK = 256

BF16 = jnp.bfloat16
F32 = jnp.float32
LOG2_E = 1.4426950408889634
NT_DIMS = (((1,), (1,)), ((), ()))
TN_DIMS = (((0,), (0,)), ((), ()))


def _dot(a, b):
    return jnp.dot(a, b, preferred_element_type=F32)


def _layer_norm(z, g, b):
    mu = jnp.mean(z, axis=-1, keepdims=True)
    zc = z - mu
    var = jnp.mean(zc * zc, axis=-1, keepdims=True)
    return zc * lax.rsqrt(var + LN_EPS) * g + b


def _rms_norm(z, g):
    return z * lax.rsqrt(jnp.mean(z * z, axis=-1, keepdims=True) + RMS_EPS) * g


def _sub(t):
    return slice(t * SUB_ROWS, (t + 1) * SUB_ROWS)


def _const_spec(shape):
    nd = len(shape)
    return pl.BlockSpec(shape, lambda *_: (0,) * nd, pipeline_mode=pl.Buffered(1))


def _row_spec(width):
    return pl.BlockSpec((STEP_ROWS, width), lambda i: (i, 0))


def _params(*sem):
    return pltpu.CompilerParams(dimension_semantics=sem, vmem_limit_bytes=VMEM_LIMIT)


def _cast_specs(jobs, n_steps, step=lambda i: i):
    in_specs, out_specs, shapes = [], [], []
    for w, layer in jobs:
        _, rows, cols = w.shape
        assert rows % n_steps == 0 and (rows // n_steps) % (2 * SUBLANES) == 0
        in_specs.append(pl.BlockSpec((None, rows // n_steps, cols),
                                     lambda *idx, layer=layer: (layer, step(*idx), 0)))
        out_specs.append(pl.BlockSpec((rows // n_steps, cols), lambda *idx: (step(*idx), 0)))
        shapes.append(jax.ShapeDtypeStruct((rows, cols), BF16))
    return in_specs, out_specs, shapes


def _cast_blocks(src_refs, dst_refs):
    for src, dst in zip(src_refs, dst_refs):
        dst[...] = src[...].astype(BF16)


def _mlp_stage(x_ref, w1_ref, w2_ref, g_ref, b_ref, o_ref, *, rows_per):
    fc = D_FF // FF_CHUNKS
    n_sub = STEP_ROWS // rows_per

    def sub(t):
        return slice(t * rows_per, (t + 1) * rows_per)

    def finish(rows, y):
        o_ref[rows, :] = _layer_norm(ALPHA * x_ref[rows, :] + y, g_ref[...], b_ref[...])

    pending = None
    for t in range(n_sub):
        xb = x_ref[sub(t), :].astype(BF16)
        y = None
        for c in range(FF_CHUNKS):
            h = _dot(xb, w1_ref[:, c * fc:(c + 1) * fc])
            h = jnp.maximum(h, 0.0)
            h = (h * h).astype(BF16)
            if t == n_sub - 1 and c == FF_CHUNKS - 1:
                piece = rows_per // TAIL_PIECES
                for r in range(TAIL_PIECES):
                    local = slice(r * piece, (r + 1) * piece)
                    y_piece = y[local, :] + _dot(h[local, :], w2_ref[c * fc:(c + 1) * fc, :])
                    finish(slice(t * rows_per + r * piece, t * rows_per + (r + 1) * piece), y_piece)
                break
            yc = _dot(h, w2_ref[c * fc:(c + 1) * fc, :])
            y = yc if y is None else y + yc
            if c == 0 and pending is not None:
                finish(*pending)
        pending = (sub(t), y)


def _sconv_stage(x_ref, win_ref, cw_ref, wout_ref, g_ref, b_ref, o_ref, ubuf, gbuf, *, steps_per_seq):
    rows_per, n_sub, sub = SUB_ROWS, SUBTILES, _sub

    @pl.when(pl.program_id(0) % steps_per_seq == 0)
    def _():
        ubuf[:, 0:SC_HALO, :] = jnp.zeros((N_SLABS, SC_HALO, LANES), F32)

    slabs_per_chunk = IN_CHUNK // LANES
    b_gates = []
    for t in range(n_sub):
        xb = x_ref[sub(t), :].astype(BF16)
        slabs = []
        for c in range(D_MODEL // IN_CHUNK):
            b_cols, c_cols, h_cols = (slice(part * D_MODEL + c * IN_CHUNK, part * D_MODEL + (c + 1) * IN_CHUNK)
                                      for part in range(3))
            b_gate = _dot(xb, win_ref[:, b_cols])
            u = _dot(xb, win_ref[:, c_cols]) * _dot(xb, win_ref[:, h_cols])
            for i in range(slabs_per_chunk):
                lanes = slice(i * LANES, (i + 1) * LANES)
                slabs.append(b_gate[:, lanes])
                ubuf[c * slabs_per_chunk + i, SC_HALO + t * rows_per:SC_HALO + (t + 1) * rows_per, :] = u[:, lanes]
        b_gates.append(slabs)

    ys = []
    for t in range(n_sub):
        for s in range(N_SLABS):
            cols = slice(s * LANES, (s + 1) * LANES)
            conv = None
            for k in range(SC_WIDTH):
                start = SC_HALO + t * rows_per - (SC_WIDTH - 1) + k
                term = cw_ref[k:k + 1, cols] * ubuf[s, start:start + rows_per, :]
                conv = term if conv is None else conv + term
            gbuf[sub(t), cols] = (b_gates[t][s] * conv).astype(BF16)
        ys.append(_dot(gbuf[sub(t), :], wout_ref[...]))

    for t in range(n_sub):
        o_ref[sub(t), :] = _layer_norm(ALPHA * x_ref[sub(t), :] + ys[t], g_ref[...], b_ref[...])

    ubuf[:, 0:SC_HALO, :] = ubuf[:, STEP_ROWS:STEP_ROWS + SC_HALO, :]


def _conf_stage(x_ref, w1_ref, b1_ref, dww_ref, dwb_ref, ng_ref, nb_ref, w2_ref, b2_ref, g_ref, b_ref,
                o_ref, hbuf, cbuf, *, steps_per_seq):
    @pl.when(pl.program_id(0) % steps_per_seq == 0)
    def _():
        hbuf[:, 0:CF_HALO, :] = jnp.zeros((N_SLABS, CF_HALO, LANES), F32)

    slabs_per_chunk = IN_CHUNK // LANES
    for t in range(SUBTILES):
        xb = x_ref[_sub(t), :].astype(BF16)
        for c in range(D_MODEL // IN_CHUNK):
            a_cols = slice(c * IN_CHUNK, (c + 1) * IN_CHUNK)
            g_cols = slice(D_MODEL + c * IN_CHUNK, D_MODEL + (c + 1) * IN_CHUNK)
            a = _dot(xb, w1_ref[:, a_cols]) + b1_ref[:, a_cols]
            gate = _dot(xb, w1_ref[:, g_cols]) + b1_ref[:, g_cols]
            h = a * jax.nn.sigmoid(gate)
            for i in range(slabs_per_chunk):
                hbuf[c * slabs_per_chunk + i, CF_HALO + t * SUB_ROWS:CF_HALO + (t + 1) * SUB_ROWS, :] = (
                    h[:, i * LANES:(i + 1) * LANES])

    for t in range(SUBTILES):
        for s in range(N_SLABS):
            cols = slice(s * LANES, (s + 1) * LANES)
            for r0 in range(t * SUB_ROWS, (t + 1) * SUB_ROWS, CONV_ROWS):
                acc = jnp.zeros((CONV_ROWS, LANES), F32) + dwb_ref[:, cols]
                for k in range(CONF_WIDTH):
                    start = CF_HALO + r0 - (CONF_WIDTH - 1) + k
                    acc = acc + dww_ref[k:k + 1, cols] * hbuf[s, start:start + CONV_ROWS, :]
                cbuf[r0:r0 + CONV_ROWS, cols] = acc
        hn = _layer_norm(cbuf[_sub(t), :], ng_ref[...], nb_ref[...])
        hs = (hn * jax.nn.sigmoid(hn)).astype(BF16)
        y = _dot(hs, w2_ref[...]) + b2_ref[...]
        o_ref[_sub(t), :] = _layer_norm(ALPHA * x_ref[_sub(t), :] + y, g_ref[...], b_ref[...])

    hbuf[:, 0:CF_HALO, :] = hbuf[:, STEP_ROWS:STEP_ROWS + CF_HALO, :]


def _attn_out_stage(x_ref, ot_ref, wo_ref, g_ref, b_ref, o_ref):
    ys = [lax.dot_general(ot_ref[0, :, _sub(t)], wo_ref[...], TN_DIMS, preferred_element_type=F32)
          for t in range(SUBTILES)]
    for t in range(SUBTILES):
        o_ref[_sub(t), :] = _layer_norm(ALPHA * x_ref[_sub(t), :] + ys[t], g_ref[...], b_ref[...])


def _sublayer_kernel(x_ref, *refs, mixer, n_mixer_in, with_mlp, mlp_sub_rows, n_cast):
    mixer_in, refs = refs[:n_mixer_in], refs[n_mixer_in:]
    n_mlp_in = 4 if with_mlp else 0
    mlp_in, refs = refs[:n_mlp_in], refs[n_mlp_in:]
    cast_src, o_ref, cast_dst, scratch = refs[:n_cast], refs[n_cast], refs[n_cast + 1:2 * n_cast + 1], refs[2 * n_cast + 1:]
    _cast_blocks(cast_src, cast_dst)
    if mixer is not None:
        mixer(x_ref, *mixer_in, o_ref, *scratch)
    if with_mlp:
        _mlp_stage(x_ref if mixer is None else o_ref, *mlp_in, o_ref, rows_per=mlp_sub_rows)


def _sublayer_call(name, x, mixer=None, mixer_args=(), mixer_specs=(), scratch=(), mlp=None, cast=(),
                   mlp_sub_rows=FINE_SUB_ROWS):
    n = x.shape[0]
    n_steps = n // STEP_ROWS
    vec = _const_spec((1, D_MODEL))
    mlp_specs = [] if mlp is None else [_const_spec((D_MODEL, D_FF)), _const_spec((D_FF, D_MODEL)), vec, vec]
    cast_in, cast_out, cast_shapes = _cast_specs(cast, n_steps)
    return pl.pallas_call(
        functools.partial(_sublayer_kernel, mixer=mixer, n_mixer_in=len(mixer_args),
                          with_mlp=mlp is not None, mlp_sub_rows=mlp_sub_rows, n_cast=len(cast)),
        grid=(n_steps,),
        in_specs=[_row_spec(D_MODEL)] + list(mixer_specs) + mlp_specs + cast_in,
        out_specs=[_row_spec(D_MODEL)] + cast_out,
        out_shape=[jax.ShapeDtypeStruct((n, D_MODEL), F32)] + cast_shapes,
        scratch_shapes=list(scratch),
        compiler_params=_params("parallel" if mixer is None else "arbitrary"),
        name=name,
    )(x, *mixer_args, *(mlp or ()), *[w for w, _ in cast])


def _sconv_mixer(w_in, conv_w, w_out, g, b, seq):
    vec = _const_spec((1, D_MODEL))
    return dict(
        mixer=functools.partial(_sconv_stage, steps_per_seq=seq // STEP_ROWS),
        mixer_args=(w_in, conv_w, w_out, g, b),
        mixer_specs=(_const_spec((D_MODEL, 3 * D_MODEL)), _const_spec((SC_WIDTH, D_MODEL)),
                     _const_spec((D_MODEL, D_MODEL)), vec, vec),
        scratch=(pltpu.VMEM((N_SLABS, SC_HALO + STEP_ROWS, LANES), F32), pltpu.VMEM((STEP_ROWS, D_MODEL), BF16)),
        mlp_sub_rows=SUB_ROWS)


def _conf_mixer(w_pw1, b_pw1, dw_w, dw_b, norm_g, norm_b, w_pw2, b_pw2, g, b, seq):
    vec = _const_spec((1, D_MODEL))
    return dict(
        mixer=functools.partial(_conf_stage, steps_per_seq=seq // STEP_ROWS),
        mixer_args=(w_pw1, b_pw1, dw_w, dw_b, norm_g, norm_b, w_pw2, b_pw2, g, b),
        mixer_specs=(_const_spec((D_MODEL, 2 * D_MODEL)), _const_spec((1, 2 * D_MODEL)),
                     _const_spec((CONF_WIDTH, D_MODEL)), vec, vec, vec, _const_spec((D_MODEL, D_MODEL)), vec, vec, vec),
        scratch=(pltpu.VMEM((N_SLABS, CF_HALO + STEP_ROWS, LANES), F32), pltpu.VMEM((STEP_ROWS, D_MODEL), F32)))


def _attn_out_mixer(ot, w_o, g, b, seq):
    sps = seq // STEP_ROWS
    hv = MLA_HEADS * V_HEAD
    vec = _const_spec((1, D_MODEL))
    return dict(
        mixer=_attn_out_stage,
        mixer_args=(ot, w_o, g, b),
        mixer_specs=(pl.BlockSpec((1, hv, STEP_ROWS), lambda i: (i // sps, 0, i % sps)),
                     _const_spec((hv, D_MODEL)), vec, vec))


def _mla_proj_kernel(x_ref, cos_ref, sin_ref, wdq_ref, gq_ref, wuq_ref, wdkv_ref, gkv_ref, wkpe_ref,
                     wuk_ref, wuvt_ref, q_ref, k_ref, vt_ref):
    nope_w = MLA_HEADS * QK_NOPE
    rope_w = MLA_HEADS * QK_ROPE
    scale = (QK_NOPE + QK_ROPE) ** -0.5 * LOG2_E
    first_half = lax.broadcasted_iota(jnp.int32, (SUB_ROWS, LANES), 1) < QK_ROPE
    for t in range(SUBTILES):
        xb = x_ref[_sub(t), :].astype(BF16)
        cos = cos_ref[_sub(t), :]
        sin = sin_ref[_sub(t), :]

        cq = _rms_norm(_dot(xb, wdq_ref[...]), gq_ref[...]).astype(BF16)
        heads_per_chunk = PROJ_CHUNK // QK_NOPE
        for c in range(nope_w // PROJ_CHUNK):
            q_nope = _dot(cq, wuq_ref[:, c * PROJ_CHUNK:(c + 1) * PROJ_CHUNK]) * scale
            for i in range(heads_per_chunk):
                hd = c * heads_per_chunk + i
                q_ref[_sub(t), hd * HEAD_W:hd * HEAD_W + LANES] = q_nope[:, i * LANES:(i + 1) * LANES].astype(BF16)
        pairs_per_chunk = PROJ_CHUNK // LANES
        for c in range(rope_w // PROJ_CHUNK):
            pe = _dot(cq, wuq_ref[:, nope_w + c * PROJ_CHUNK:nope_w + (c + 1) * PROJ_CHUNK])
            pe_rot = _dot(cq, wuq_ref[:, nope_w + rope_w + c * PROJ_CHUNK:nope_w + rope_w + (c + 1) * PROJ_CHUNK])
            for i in range(pairs_per_chunk):
                lanes = slice(i * LANES, (i + 1) * LANES)
                slab = ((pe[:, lanes] * cos + pe_rot[:, lanes] * sin) * scale).astype(BF16)
                pair = c * pairs_per_chunk + i
                for hd in (2 * pair, 2 * pair + 1):
                    q_ref[_sub(t), hd * HEAD_W + LANES:(hd + 1) * HEAD_W] = slab

        ckv = _rms_norm(_dot(xb, wdkv_ref[...]), gkv_ref[...]).astype(BF16)
        kpe2 = _dot(xb, wkpe_ref[...])
        k_pe = kpe2[:, 0:LANES] * cos + kpe2[:, LANES:2 * LANES] * sin
        k_pe_even = jnp.where(first_half, k_pe, 0.0).astype(BF16)
        k_pe_odd = jnp.where(first_half, 0.0, k_pe).astype(BF16)
        for c in range(nope_w // PROJ_CHUNK):
            k_nope = _dot(ckv, wuk_ref[:, c * PROJ_CHUNK:(c + 1) * PROJ_CHUNK])
            for i in range(heads_per_chunk):
                hd = c * heads_per_chunk + i
                k_ref[_sub(t), hd * HEAD_W:hd * HEAD_W + LANES] = k_nope[:, i * LANES:(i + 1) * LANES].astype(BF16)
                k_ref[_sub(t), hd * HEAD_W + LANES:(hd + 1) * HEAD_W] = k_pe_even if hd % 2 == 0 else k_pe_odd
        for c in range(MLA_HEADS * V_HEAD // PROJ_CHUNK):
            rows = slice(c * PROJ_CHUNK, (c + 1) * PROJ_CHUNK)
            vt = lax.dot_general(wuvt_ref[rows, :], ckv, NT_DIMS, preferred_element_type=F32)
            vt_ref[0, rows, _sub(t)] = vt.astype(BF16)


def _mla_proj(x, cos, sin, wdq, gq, wuq, wdkv, gkv, wkpe, wuk, wuvt, batch, seq):
    n = x.shape[0]
    sps = seq // STEP_ROWS
    hw = MLA_HEADS * HEAD_W
    hv = MLA_HEADS * V_HEAD
    pos_spec = pl.BlockSpec((STEP_ROWS, LANES), lambda i: (i % sps, 0))
    return pl.pallas_call(
        _mla_proj_kernel,
        grid=(n // STEP_ROWS,),
        in_specs=[_row_spec(D_MODEL), pos_spec, pos_spec,
                  _const_spec((D_MODEL, Q_LORA)), _const_spec((1, Q_LORA)),
                  _const_spec((Q_LORA, MLA_HEADS * (QK_NOPE + 2 * QK_ROPE))),
                  _const_spec((D_MODEL, KV_LORA)), _const_spec((1, KV_LORA)),
                  _const_spec((D_MODEL, 2 * LANES)),
                  _const_spec((KV_LORA, MLA_HEADS * QK_NOPE)), _const_spec((hv, KV_LORA))],
        out_specs=[_row_spec(hw), _row_spec(hw),
                   pl.BlockSpec((1, hv, STEP_ROWS), lambda i: (i // sps, 0, i % sps))],
        out_shape=[jax.ShapeDtypeStruct((n, hw), BF16), jax.ShapeDtypeStruct((n, hw), BF16),
                   jax.ShapeDtypeStruct((batch, hv, seq), BF16)],
        compiler_params=_params("parallel"),
        name="mla_proj",
    )(x, cos, sin, wdq, gq, wuq, wdkv, gkv, wkpe, wuk, wuvt)


def _mla_attn_kernel(q_ref, k_ref, vt_ref, *refs, seq, n_cast):
    cast_src, o_ref, cast_dst = refs[:n_cast], refs[n_cast], refs[n_cast + 1:]
    _cast_blocks(cast_src, cast_dst)
    key_chunk = lax.broadcasted_iota(jnp.int32, (ATT_TQ, ATT_TQ), 0) // CHUNK
    qry_chunk = lax.broadcasted_iota(jnp.int32, (ATT_TQ, ATT_TQ), 1) // CHUNK
    diag_allowed = key_chunk <= qry_chunk

    def scores(hd, qi):
        lo, hi = qi * ATT_TQ, (qi + 1) * ATT_TQ
        hcols = slice(hd * HEAD_W, (hd + 1) * HEAD_W)
        q = q_ref[0, lo:hi, hcols]
        s_diag = lax.dot_general(k_ref[0, lo:hi, hcols], q, NT_DIMS, preferred_element_type=F32)
        s_diag = jnp.where(diag_allowed, s_diag, -1e30)
        m = jnp.max(s_diag, axis=0, keepdims=True)
        s_off = None
        if qi > 0:
            s_off = lax.dot_general(k_ref[0, 0:lo, hcols], q, NT_DIMS, preferred_element_type=F32)
            m = jnp.maximum(m, jnp.max(s_off, axis=0, keepdims=True))
        return s_diag, s_off, m

    def finish(hd, qi, s_diag, s_off, m):
        lo, hi = qi * ATT_TQ, (qi + 1) * ATT_TQ
        vrows = slice(hd * V_HEAD, (hd + 1) * V_HEAD)
        p_diag = jnp.exp2(s_diag - m)
        l = jnp.sum(p_diag, axis=0, keepdims=True)
        acc = _dot(vt_ref[0, vrows, lo:hi], p_diag.astype(BF16))
        if qi > 0:
            p_off = jnp.exp2(s_off - m)
            l = l + jnp.sum(p_off, axis=0, keepdims=True)
            acc = acc + _dot(vt_ref[0, vrows, 0:lo], p_off.astype(BF16))
        o_ref[0, vrows, lo:hi] = (acc / l).astype(BF16)

    items = [(hd, qi) for qi in range(seq // ATT_TQ) for hd in range(ATT_HEADS)]
    pending = scores(*items[0])
    for n, item in enumerate(items):
        nxt = scores(*items[n + 1]) if n + 1 < len(items) else None
        finish(*item, *pending)
        pending = nxt


def _mla_attn(q, k, vt, batch, seq, cast=()):
    assert ATT_TQ % CHUNK == 0 and seq % ATT_TQ == 0 and MLA_HEADS % ATT_HEADS == 0
    head_groups = MLA_HEADS // ATT_HEADS
    qk_spec = pl.BlockSpec((1, seq, ATT_HEADS * HEAD_W), lambda b, h: (b, 0, h))
    vt_spec = pl.BlockSpec((1, ATT_HEADS * V_HEAD, seq), lambda b, h: (b, h, 0))
    cast_in, cast_out, cast_shapes = _cast_specs(cast, batch * head_groups, lambda b, h: b * head_groups + h)
    return pl.pallas_call(
        functools.partial(_mla_attn_kernel, seq=seq, n_cast=len(cast)),
        grid=(batch, head_groups),
        in_specs=[qk_spec, qk_spec, vt_spec] + cast_in,
        out_specs=[vt_spec] + cast_out,
        out_shape=[jax.ShapeDtypeStruct((batch, MLA_HEADS * V_HEAD, seq), BF16)] + cast_shapes,
        compiler_params=_params("parallel", "parallel"),
        name="mla_attn",
    )(q.reshape(batch, seq, MLA_HEADS * HEAD_W), k.reshape(batch, seq, MLA_HEADS * HEAD_W), vt,
      *[w for w, _ in cast])


def _rope_tables(seq):
    pos = jnp.arange(seq, dtype=F32)
    inv_freq = ROPE_THETA ** (-jnp.arange(0, QK_ROPE, 2, dtype=F32) / QK_ROPE)
    ang = pos[:, None] * inv_freq[None, :]
    reps = LANES // (QK_ROPE // 2)
    return jnp.tile(jnp.cos(ang), (1, reps)), jnp.tile(jnp.sin(ang), (1, reps))


def _rotate_half_cols(w):
    half = w.shape[-1] // 2
    return jnp.concatenate([-w[..., half:], w[..., :half]], axis=-1)


def _mla_attention(x, w_dq, g_q, w_uq, w_dkv, g_kv, w_uk, w_uv, batch, seq, cast=()):
    wq = w_uq.reshape(Q_LORA, MLA_HEADS, QK_NOPE + QK_ROPE)
    wq_pe = wq[..., QK_NOPE:]
    wuq = jnp.concatenate(
        [wq[..., :QK_NOPE].reshape(Q_LORA, -1), wq_pe.reshape(Q_LORA, -1),
         _rotate_half_cols(wq_pe).reshape(Q_LORA, -1)], axis=1).astype(BF16)
    wk_pe = w_dkv[:, KV_LORA:]
    wk_pe_rot = _rotate_half_cols(wk_pe)
    wkpe = jnp.concatenate([wk_pe, wk_pe, wk_pe_rot, wk_pe_rot], axis=1).astype(BF16)
    wuvt = w_uv.reshape(KV_LORA, MLA_HEADS * V_HEAD).T.astype(BF16)
    cos, sin = _rope_tables(seq)
    q, k, vt = _mla_proj(x, cos, sin, w_dq.astype(BF16), g_q[None], wuq, w_dkv[:, :KV_LORA].astype(BF16),
                         g_kv[None], wkpe, w_uk.reshape(KV_LORA, -1).astype(BF16), wuvt, batch, seq)
    return _mla_attn(q, k, vt, batch, seq, cast=cast)


def kernel(x, sc_w_in, sc_conv_w, sc_w_out, mla_w_dq, mla_g_q, mla_w_uq, mla_w_dkv, mla_g_kv, mla_w_uk, mla_w_uv, mla_w_o, cf_w_pw1, cf_b_pw1, cf_dw_w, cf_dw_b, cf_norm_g, cf_norm_b, cf_w_pw2, cf_b_pw2, ff_w1, ff_w2, ln_mix_g, ln_mix_b, ln_ff_g, ln_ff_b):
    batch, seq, d = x.shape
    assert d == D_MODEL and seq % STEP_ROWS == 0
    h = x.reshape(batch * seq, d)

    calls = []
    for i in range(DEPTH):
        m, j = i % N_MIXERS, i // N_MIXERS
        mlp_w = [(ff_w1, i), (ff_w2, i)]
        if m == 1:
            calls.append(("attention", i, [], True))
            calls.append(("attn_out+mlp", i, mlp_w, False))
        elif m == 0 and i > 0:
            calls.append(("sconv+mlp", i, [(sc_w_in, j), (sc_w_out, j)] + mlp_w, False))
        else:
            mixer_w = [(sc_w_in, j), (sc_w_out, j)] if m == 0 else [(cf_w_pw1, j), (cf_w_pw2, j)]
            calls.append(("sconv" if m == 0 else "conf", i, mixer_w, i == 0))
            calls.append(("mlp", i, mlp_w, i == 0))
    jobs = [[] for _ in calls]
    source = [None] * len(calls)
    carrier = None
    for c, (_, _, needed, can_carry) in enumerate(calls):
        if needed and carrier is not None:
            source[c] = (carrier, len(jobs[carrier]))
            jobs[carrier].extend(needed)
        if can_carry:
            carrier = c
    casted = [None] * len(calls)

    def bf16_weights(c):
        needed = calls[c][2]
        if source[c] is None:
            return [w[layer].astype(BF16) for w, layer in needed]
        carrier, first = source[c]
        return casted[carrier][first:first + len(needed)]

    attn = None
    for c, (kind, i, _, _) in enumerate(calls):
        j = i // N_MIXERS
        g, b = ln_mix_g[i][None], ln_mix_b[i][None]
        weights = bf16_weights(c)
        mixer = {}
        if kind.startswith("sconv"):
            mixer = _sconv_mixer(weights[0], sc_conv_w[j], weights[1], g, b, seq)
        elif kind.startswith("conf"):
            mixer = _conf_mixer(weights[0], cf_b_pw1[j][None], cf_dw_w[j], cf_dw_b[j][None], cf_norm_g[j][None],
                                cf_norm_b[j][None], weights[1], cf_b_pw2[j][None], g, b, seq)
        elif kind.startswith("attn_out"):
            mixer = _attn_out_mixer(attn, mla_w_o[j].astype(BF16), g, b, seq)
        if kind == "attention":
            attn, *casted[c] = _mla_attention(h, mla_w_dq[j], mla_g_q[j], mla_w_uq[j], mla_w_dkv[j], mla_g_kv[j],
                                              mla_w_uk[j], mla_w_uv[j], batch, seq, cast=jobs[c])
            continue
        mlp = (*weights[-2:], ln_ff_g[i][None], ln_ff_b[i][None]) if kind.endswith("mlp") else None
        h, *casted[c] = _sublayer_call(kind.replace("+", "_") + "_ln", h, mlp=mlp, cast=jobs[c], **mixer)
    return h.reshape(batch, seq, d)
```
